```python
import jax, jax.numpy as jnp
from jax import lax
import numpy as np

D_MODEL = 1024
BATCH = 4
SEQ = 8192
DEPTH = 2

HEAD_DIM = 64
MOBA_HEADS = 6
MOBA_BLOCK = 256
MOBA_TOPK = 3
MOBA_QBLOCK = 64
CONV_CH = 256
CONV_WIDTH = 31
DSA_HEADS = 6
DSA_TOPK = 256
DSA_QBLOCK = 128
IDX_HEADS = 8
IDX_DIM = 64
MOBA_W = 384
DSA_W = 384
MIX_W = 1024
SPLIT_SIZES = (384, 384, 384, 256, 256, 384, 384, 384, 512, 64, 8)
N_IN = 3400
N_GROUPS = 4
EXPERTS_PER_GROUP = 8
N_EXPERTS = 32
EXPERT_TOPK = 2
D_EXPERT = 512
MOE_BLOCK = 256
EPS = 1e-6

kernel_name = "hymba_moba_conformer_dsa_hmoe_adaln"


def rms_norm(x, g):
    xf = x.astype(jnp.float32)
    y = xf * lax.rsqrt(jnp.mean(xf * xf, axis=-1, keepdims=True) + EPS)
    return (y * g.astype(jnp.float32)).astype(x.dtype)


def layer_norm(x, g, b):
    xf = x.astype(jnp.float32)
    mu = jnp.mean(xf, axis=-1, keepdims=True)
    xc = xf - mu
    y = xc * lax.rsqrt(jnp.mean(xc * xc, axis=-1, keepdims=True) + EPS)
    return (y * g.astype(jnp.float32) + b.astype(jnp.float32)).astype(x.dtype)


def moba_attention(q, k, v):
    B, S, H, Dh = q.shape
    nb = -(-S // MOBA_BLOCK)
    pad = nb * MOBA_BLOCK - S
    to_blocks = lambda t: jnp.pad(t, ((0, 0), (0, pad), (0, 0), (0, 0))).reshape(
        B, nb, MOBA_BLOCK, H, Dh).transpose(0, 3, 1, 2, 4)
    kb, vb = to_blocks(k), to_blocks(v)
    k_mean = jnp.mean(kb.astype(jnp.float32), axis=3).astype(q.dtype)
    topb = min(MOBA_TOPK, nb)
    n_sel = topb * MOBA_BLOCK
    nq = S // MOBA_QBLOCK
    QB = MOBA_QBLOCK
    scale = HEAD_DIM ** -0.5
    gather_blocks = jax.vmap(jax.vmap(lambda blocks, idx: blocks[idx]))
    qs = q.reshape(B, nq, QB, H, Dh).transpose(1, 0, 3, 2, 4)

    def one_block(args):
        qb, i = args
        t = i * QB + jnp.arange(QB)
        own = (i * QB) // MOBA_BLOCK
        gate = jnp.einsum('bhqd,bhnd->bhqn', qb, k_mean).astype(jnp.float32)
        gate = jnp.where(jnp.arange(nb) < own, gate, -jnp.inf)
        _, sel = lax.top_k(gate, topb)
        k_sel = gather_blocks(kb, sel)
        v_sel = gather_blocks(vb, sel)
        k_own = lax.dynamic_index_in_dim(kb, own, axis=2, keepdims=False)
        v_own = lax.dynamic_index_in_dim(vb, own, axis=2, keepdims=False)
        s_sel = jnp.einsum('bhqd,bhqjkd->bhqjk', qb, k_sel).astype(jnp.float32) * scale
        s_sel = jnp.where((jnp.arange(topb) < own)[:, None], s_sel, -jnp.inf)
        s_own = jnp.einsum('bhqd,bhkd->bhqk', qb, k_own).astype(jnp.float32) * scale
        kpos = own * MOBA_BLOCK + jnp.arange(MOBA_BLOCK)
        s_own = jnp.where(kpos[None, :] <= t[:, None], s_own, -jnp.inf)
        p = jax.nn.softmax(jnp.concatenate([s_sel.reshape(B, H, QB, n_sel), s_own], axis=-1),
                           axis=-1).astype(v.dtype)
        out = (jnp.einsum('bhqjk,bhqjkd->bhqd', p[..., :n_sel].reshape(B, H, QB, topb, MOBA_BLOCK), v_sel)
               + jnp.einsum('bhqk,bhkd->bhqd', p[..., n_sel:], v_own))
        return out

    out = lax.map(one_block, (qs, jnp.arange(nq)))
    return out.transpose(1, 0, 3, 2, 4).reshape(B, S, H * Dh)


def conformer_conv(a, g, conv_w, conv_b, ln_g, ln_b):
    u = a * jax.nn.sigmoid(g)
    u = jnp.pad(u, ((0, 0), (CONV_WIDTH - 1, 0), (0, 0)))
    y = lax.conv_general_dilated(u, conv_w[:, None, :], window_strides=(1,), padding='VALID',
                                 dimension_numbers=('NWC', 'WIO', 'NWC'),
                                 feature_group_count=CONV_CH) + conv_b
    return jax.nn.silu(layer_norm(y, ln_g, ln_b))


def dsa_attention(q, k, v, q_idx, k_idx, w_idx):
    B, S, H, Dh = q.shape
    topk = min(DSA_TOPK, S // 4)
    QB = DSA_QBLOCK
    nq = S // QB
    k_idx32 = k_idx.astype(jnp.float32)
    w32 = w_idx.astype(jnp.float32) * (IDX_HEADS ** -0.5 * IDX_DIM ** -0.5)
    split = lambda t: t.reshape(B, nq, QB, *t.shape[2:]).swapaxes(0, 1)
    gather_keys = jax.vmap(lambda kv, idx: kv[idx])
    kpos = jnp.arange(S)

    def one_block(args):
        qb, qib, wb, i = args
        t = i * QB + jnp.arange(QB)
        score = jnp.einsum('bqh,bqhs->bqs', wb,
                           jax.nn.relu(jnp.einsum('bqhd,bsd->bqhs', qib.astype(jnp.float32), k_idx32)))
        score = jnp.where(kpos[None, :] <= t[:, None], score, -jnp.inf)
        _, idx = lax.top_k(score, topk)
        valid = idx <= t[None, :, None]
        k_sel = gather_keys(k, idx)
        v_sel = gather_keys(v, idx)
        s = jnp.einsum('bqhd,bqkhd->bhqk', qb, k_sel).astype(jnp.float32) * HEAD_DIM ** -0.5
        s = jnp.where(valid[:, None], s, -jnp.inf)
        p = jax.nn.softmax(s, axis=-1).astype(v.dtype)
        return jnp.einsum('bhqk,bqkhd->bqhd', p, v_sel)

    out = lax.map(one_block, (split(q), split(q_idx), split(w32), jnp.arange(nq)))
    return out.swapaxes(0, 1).reshape(B, S, H * Dh)


def hier_moe(h, rg_w, rg_b, re_w, re_b, w1, w3, w2):
    T, D = h.shape
    g_logits = (h @ rg_w + rg_b).astype(jnp.float32)
    g_idx = jnp.argmax(g_logits, axis=-1)
    g_w = jnp.max(jax.nn.softmax(g_logits, axis=-1), axis=-1)
    e_all = (h @ re_w + re_b).astype(jnp.float32).reshape(T, N_GROUPS, EXPERTS_PER_GROUP)
    e_logits = jnp.take_along_axis(e_all, g_idx[:, None, None], axis=1)[:, 0]
    e_top, e_loc = lax.top_k(e_logits, EXPERT_TOPK)
    gates = g_w[:, None] * jax.nn.softmax(e_top, axis=-1)
    expert = g_idx[:, None] * EXPERTS_PER_GROUP + e_loc
    n_slots = T * EXPERT_TOPK
    flat_e = expert.reshape(-1)
    order = jnp.argsort(flat_e)
    sorted_e = flat_e[order]
    counts = jnp.bincount(flat_e, length=N_EXPERTS)
    padded = (counts + MOE_BLOCK - 1) // MOE_BLOCK * MOE_BLOCK
    pad_end = jnp.cumsum(padded)
    pad_start = pad_end - padded
    start = jnp.cumsum(counts) - counts
    dest = pad_start[sorted_e] + jnp.arange(n_slots) - start[sorted_e]
    P = n_slots + N_EXPERTS * MOE_BLOCK
    nblk = P // MOE_BLOCK
    buf = jnp.zeros((P, D), h.dtype).at[dest].set(h[order // EXPERT_TOPK])
    blk_e = jnp.minimum(jnp.searchsorted(pad_end, jnp.arange(nblk) * MOE_BLOCK, side='right'),
                        N_EXPERTS - 1)

    def expert_block(args):
        xb, e = args
        return (jax.nn.silu(xb @ w1[e]) * (xb @ w3[e])) @ w2[e]

    yb = lax.map(expert_block, (buf.reshape(nblk, MOE_BLOCK, D), blk_e)).reshape(P, D)
    y_slots = jnp.zeros((n_slots, D), h.dtype).at[order].set(yb[dest])
    return jnp.sum(y_slots.reshape(T, EXPERT_TOPK, D) * gates[..., None].astype(h.dtype), axis=1)


def hybrid_layer(x, c, ada_w, ada_b, n1g, w_in, conv_w, conv_b, ln_g, ln_b, mg, dg, w_out,
                 n2g, rgw, rgb, rew, reb, w1, w3, w2):
    B, S, D = x.shape
    mod = jax.nn.silu(c) @ ada_w + ada_b
    sh1, sc1, g1, sh2, sc2, g2 = [m[:, None, :] for m in jnp.split(mod, 6, axis=-1)]
    h = rms_norm(x, n1g) * (1 + sc1) + sh1
    proj = h @ w_in
    qm, km, vm, ca, cg, qd, kd, vd, qi, ki, wi = jnp.split(
        proj, np.cumsum(SPLIT_SIZES)[:-1].tolist(), axis=-1)
    heads = lambda t, n: t.reshape(B, S, n, HEAD_DIM)
    y_moba = moba_attention(heads(qm, MOBA_HEADS), heads(km, MOBA_HEADS), heads(vm, MOBA_HEADS))
    y_conv = conformer_conv(ca, cg, conv_w, conv_b, ln_g, ln_b)
    y_dsa = dsa_attention(heads(qd, DSA_HEADS), heads(kd, DSA_HEADS), heads(vd, DSA_HEADS),
                          qi.reshape(B, S, IDX_HEADS, IDX_DIM), ki, wi)
    mix = jnp.concatenate([rms_norm(y_moba, mg), y_conv, rms_norm(y_dsa, dg)], axis=-1)
    x = x + g1 * (mix @ w_out)
    h2 = rms_norm(x, n2g) * (1 + sc2) + sh2
    y = hier_moe(h2.reshape(B * S, D), rgw, rgb, rew, reb, w1, w3, w2).reshape(B, S, D)
    return x + g2 * y


def setup_inputs(seed: int = 0) -> dict:
    key = jax.random.key(seed)
    ks = jax.random.split(key, 24)
    L, D = DEPTH, D_MODEL
    nrm = lambda k, shape, s: jax.random.normal(k, shape, jnp.float32) * s
    return {
        "x": nrm(ks[0], (BATCH, SEQ, D), 1.0),
        "c": nrm(ks[1], (BATCH, D), 1.0),
        "ada_w": nrm(ks[2], (L, D, 6 * D), 0.5 * D ** -0.5),
        "ada_b": nrm(ks[3], (L, 6 * D), 0.01),
        "norm1_g": 1.0 + nrm(ks[4], (L, D), 0.01),
        "w_in": nrm(ks[5], (L, D, N_IN), D ** -0.5),
        "conv_w": nrm(ks[6], (L, CONV_WIDTH, CONV_CH), CONV_WIDTH ** -0.5),
        "conv_b": nrm(ks[7], (L, CONV_CH), 0.01),
        "conv_ln_g": 1.0 + nrm(ks[8], (L, CONV_CH), 0.01),
        "conv_ln_b": nrm(ks[9], (L, CONV_CH), 0.01),
        "moba_norm_g": 1.0 + nrm(ks[10], (L, MOBA_W), 0.01),
        "dsa_norm_g": 1.0 + nrm(ks[11], (L, DSA_W), 0.01),
        "w_out": nrm(ks[12], (L, MIX_W, D), MIX_W ** -0.5),
        "norm2_g": 1.0 + nrm(ks[13], (L, D), 0.01),
        "router_group_w": nrm(ks[14], (L, D, N_GROUPS), D ** -0.5),
        "router_group_b": nrm(ks[15], (L, N_GROUPS), 0.01),
        "router_expert_w": nrm(ks[16], (L, D, N_EXPERTS), D ** -0.5),
        "router_expert_b": nrm(ks[17], (L, N_EXPERTS), 0.01),
        "expert_w1": nrm(ks[18], (L, N_EXPERTS, D, D_EXPERT), D ** -0.5),
        "expert_w3": nrm(ks[19], (L, N_EXPERTS, D, D_EXPERT), D ** -0.5),
        "expert_w2": nrm(ks[20], (L, N_EXPERTS, D_EXPERT, D), D_EXPERT ** -0.5),
        "final_g": 1.0 + nrm(ks[21], (D,), 0.01),
    }


def reference(x, c, ada_w, ada_b, norm1_g, w_in, conv_w, conv_b, conv_ln_g, conv_ln_b,
              moba_norm_g, dsa_norm_g, w_out, norm2_g, router_group_w, router_group_b,
              router_expert_w, router_expert_b, expert_w1, expert_w3, expert_w2, final_g):
    for l in range(DEPTH):
        x = hybrid_layer(x, c, ada_w[l], ada_b[l], norm1_g[l], w_in[l], conv_w[l], conv_b[l],
                         conv_ln_g[l], conv_ln_b[l], moba_norm_g[l], dsa_norm_g[l], w_out[l],
                         norm2_g[l], router_group_w[l], router_group_b[l], router_expert_w[l],
                         router_expert_b[l], expert_w1[l], expert_w3[l], expert_w2[l])
    return rms_norm(x, final_g)
```

```python
import functools

import jax
import jax.numpy as jnp
from jax import lax
from jax.experimental import pallas as pl
from jax.experimental.pallas import tpu as pltpu

F32 = jnp.float32
I32 = jnp.int32
MXU_DTYPE = jnp.bfloat16

HEAD_DIM = 64
MOBA_HEADS = 6
MOBA_BLOCK = 256
MOBA_TOPK = 3
CONV_CH = 256
CONV_WIDTH = 31
DSA_HEADS = 6
DSA_TOPK = 256
IDX_HEADS = 8
IDX_DIM = 64
ATT_W = 384
N_GROUPS = 4
EXPERTS_PER_GROUP = 8
N_EXPERTS = 32
D_EXPERT = 512
MOE_BLOCK = 256
EPS = 1e-6

LANES = 128
NEG = -1e30
INT_MIN = -2 ** 31
VMEM_LIMIT = 56 * 1024 * 1024
Q_SCALE = HEAD_DIM ** -0.5
IDX_SCALE = IDX_HEADS ** -0.5 * IDX_DIM ** -0.5


def _params(sem):
    return pltpu.CompilerParams(dimension_semantics=sem, vmem_limit_bytes=VMEM_LIMIT)


def _rms(xf):
    return xf * lax.rsqrt(jnp.mean(xf * xf, axis=-1, keepdims=True) + EPS)


def _dot(a, b):
    return jnp.dot(a, b, preferred_element_type=F32)


def _dot_nt(a, b):
    return lax.dot_general(a, b, (((1,), (1,)), ((), ())), preferred_element_type=F32)


def _mod_kernel(c_ref, w_ref, b_ref, o_ref):
    c = c_ref[...]
    o_ref[0] = _dot(c * jax.nn.sigmoid(c), w_ref[0]) + b_ref[0]


def _modulation(c, ada_w, ada_b):
    L, D, N = ada_w.shape
    B = c.shape[0]
    rows = 8
    cp = jnp.zeros((rows, D), F32).at[:B].set(c)
    tn = 1536
    out = pl.pallas_call(
        _mod_kernel,
        grid=(L, N // tn),
        in_specs=[pl.BlockSpec((rows, D), lambda l, j: (0, 0)),
                  pl.BlockSpec((1, D, tn), lambda l, j: (l, 0, j)),
                  pl.BlockSpec((1, 1, tn), lambda l, j: (l, 0, j))],
        out_specs=pl.BlockSpec((1, rows, tn), lambda l, j: (l, 0, j)),
        out_shape=jax.ShapeDtypeStruct((L, rows, N), F32),
        compiler_params=_params(("arbitrary", "arbitrary")),
        name="adaln_mod",
    )(cp, ada_w, ada_b.reshape(L, 1, N))
    return out[:, :B]


def _inproj_kernel(*refs, combine, tm):
    if combine:
        x_ref, y_ref, g2_ref = refs[:3]
        refs = refs[3:]
    else:
        x_ref = refs[0]
        refs = refs[1:]
    (g_ref, sc_ref, sh_ref, wqm, wkm, wkmT, wvm, wglu, wqd, wkdT, wvd, wqi, wkiT, wwi) = refs[:14]
    outs = refs[14:]
    if combine:
        xo_ref = outs[0]
        outs = outs[1:]
    (qm_ref, kmT_ref, vm_ref, kmean_ref, u_ref, qd_ref, kdT_ref, vd_ref,
     qi_ref, kiT_ref, wi_ref) = outs

    x = x_ref[0]
    if combine:
        x = x + g2_ref[0] * (y_ref[0, 0] + y_ref[1, 0])
        xo_ref[0] = x
    h = _rms(x) * g_ref[...] * (1.0 + sc_ref[0]) + sh_ref[0]
    hb = h.astype(MXU_DTYPE)

    qm_ref[0] = _dot(hb, wqm[...]).astype(qm_ref.dtype)
    kmT_ref[0] = _dot_nt(wkmT[...], hb).astype(kmT_ref.dtype)
    vm_ref[0] = _dot(hb, wvm[...]).astype(vm_ref.dtype)
    k_rows = _dot(hb, wkm[...])
    for c in range(tm // MOBA_BLOCK):
        kmean_ref[0, c] = jnp.mean(k_rows[c * MOBA_BLOCK:(c + 1) * MOBA_BLOCK], axis=0, keepdims=True)
    glu = _dot(hb, wglu[...])
    u_ref[0] = glu[:, :CONV_CH] * jax.nn.sigmoid(glu[:, CONV_CH:])
    qd_ref[0] = _dot(hb, wqd[...]).astype(qd_ref.dtype)
    kdT_ref[0] = _dot_nt(wkdT[...], hb).astype(kdT_ref.dtype)
    vd_ref[0] = _dot(hb, wvd[...]).astype(vd_ref.dtype)
    qi_ref[0] = _dot(hb, wqi[...]).astype(qi_ref.dtype)
    kiT_ref[0] = _dot_nt(wkiT[...], hb).astype(kiT_ref.dtype)
    wi_ref[0] = _dot(hb, wwi[...])[:, :IDX_HEADS] * IDX_SCALE


def _split_w_in(w_in):
    D = w_in.shape[0]
    offs = [0, 384, 768, 1152, 1408, 1664, 2048, 2432, 2816, 3328, 3392, 3400]
    piece = lambda k: w_in[:, offs[k]:offs[k + 1]]
    c = lambda w: w.astype(MXU_DTYPE)
    wwi = jnp.zeros((D, LANES), F32).at[:, :IDX_HEADS].set(piece(10))
    return dict(
        wqm=c(piece(0) * Q_SCALE), wkm=c(piece(1)), wkmT=c(piece(1).T), wvm=c(piece(2)),
        wglu=c(w_in[:, offs[3]:offs[5]]),
        wqd=c(piece(5) * Q_SCALE), wkdT=c(piece(6).T), wvd=c(piece(7)),
        wqi=c(piece(8)), wkiT=c(piece(9).T), wwi=c(wwi))


def _inproj(x, comb, n1g, sc1, sh1, w):
    B, S, D = x.shape
    tm = 512
    nb_t = tm // MOBA_BLOCK
    combine = comb is not None
    row = lambda width: pl.BlockSpec((1, tm, width), lambda b, i: (b, i, 0))
    col = lambda height: pl.BlockSpec((1, height, tm), lambda b, i: (b, 0, i))
    vec = pl.BlockSpec((1, 1, D), lambda b, i: (b, 0, 0))
    full = lambda a: pl.BlockSpec(a.shape, lambda b, i: (0,) * a.ndim)

    ins, in_specs = [x], [row(D)]
    if combine:
        y, g2 = comb
        ins += [y, g2]
        in_specs += [pl.BlockSpec((2, 1, tm, D), lambda b, i: (0, b, i, 0)), vec]
    names = ["wqm", "wkm", "wkmT", "wvm", "wglu", "wqd", "wkdT", "wvd", "wqi", "wkiT", "wwi"]
    ins += [n1g.reshape(1, D), sc1, sh1] + [w[k] for k in names]
    in_specs += [pl.BlockSpec((1, D), lambda b, i: (0, 0)), vec, vec] + [full(w[k]) for k in names]

    sds = jax.ShapeDtypeStruct
    out_shape, out_specs = [], []
    if combine:
        out_shape.append(sds((B, S, D), F32))
        out_specs.append(row(D))
    out_shape += [
        sds((B, S, ATT_W), MXU_DTYPE), sds((B, ATT_W, S), MXU_DTYPE), sds((B, S, ATT_W), MXU_DTYPE),
        sds((B, S // MOBA_BLOCK, 1, ATT_W), F32),
        sds((B, S, CONV_CH), F32),
        sds((B, S, ATT_W), MXU_DTYPE), sds((B, ATT_W, S), MXU_DTYPE), sds((B, S, ATT_W), MXU_DTYPE),
        sds((B, S, IDX_HEADS * IDX_DIM), MXU_DTYPE), sds((B, IDX_DIM, S), MXU_DTYPE),
        sds((B, S, IDX_HEADS), F32)]
    out_specs += [
        row(ATT_W), col(ATT_W), row(ATT_W),
        pl.BlockSpec((1, nb_t, 1, ATT_W), lambda b, i: (b, i, 0, 0)),
        row(CONV_CH),
        row(ATT_W), col(ATT_W), row(ATT_W),
        row(IDX_HEADS * IDX_DIM), col(IDX_DIM), row(IDX_HEADS)]

    outs = pl.pallas_call(
        functools.partial(_inproj_kernel, combine=combine, tm=tm),
        grid=(B, S // tm),
        in_specs=in_specs, out_specs=out_specs, out_shape=out_shape,
        compiler_params=_params(("arbitrary", "arbitrary")),
        name="inproj",
    )(*ins)
    if combine:
        return outs[0], outs[1:]
    return x, outs


def _softmax_step(s, v_tile, m, l, acc):
    m_new = jnp.maximum(m, jnp.max(s, axis=-1, keepdims=True))
    alpha = jnp.exp(m - m_new)
    p = jnp.exp(s - m_new)
    l = alpha * l + jnp.sum(p, axis=-1, keepdims=True)
    acc = alpha * acc + _dot(p.astype(v_tile.dtype), v_tile)
    return m_new, l, acc


def _moba_kernel(q_ref, kT_ref, v_ref, kmT_ref, o_ref, *, nb, nb_lanes):
    TQ = MOBA_BLOCK
    i = pl.program_id(1)
    lane = lax.broadcasted_iota(I32, (TQ, nb_lanes), 1)
    row = lax.broadcasted_iota(I32, (TQ, TQ), 0)
    colk = lax.broadcasted_iota(I32, (TQ, TQ), 1)
    past = lane < i

    for h in range(MOBA_HEADS):
        hs = slice(h * HEAD_DIM, (h + 1) * HEAD_DIM)
        q = q_ref[0, :, hs]
        gate = _dot(q, kmT_ref[0, hs, :])
        gate = jnp.where(past, gate, -jnp.inf)
        rank = jnp.zeros((TQ, nb_lanes), I32)
        for mblk in range(nb):
            gm = gate[:, mblk:mblk + 1]
            before = jnp.where(gm > gate, 1, jnp.where(gm == gate, jnp.where(mblk < lane, 1, 0), 0))
            rank = rank + before
        sel_bias = jnp.where(past, jnp.where(rank < MOBA_TOPK, 0.0, NEG), NEG)

        def past_block(j, carry):
            m, l, acc = carry
            ks = pl.ds(pl.multiple_of(j * TQ, TQ), TQ)
            bias = jnp.sum(jnp.where(lane == j, sel_bias, 0.0), axis=-1, keepdims=True)
            s = _dot(q, kT_ref[0, hs, ks]) + bias
            return _softmax_step(s, v_ref[0, ks, hs], m, l, acc)

        init = (jnp.full((TQ, 1), NEG, F32), jnp.zeros((TQ, 1), F32), jnp.zeros((TQ, HEAD_DIM), F32))
        m, l, acc = lax.fori_loop(0, i, past_block, init)
        ks = pl.ds(pl.multiple_of(i * TQ, TQ), TQ)
        s = jnp.where(colk <= row, _dot(q, kT_ref[0, hs, ks]), NEG)
        m, l, acc = _softmax_step(s, v_ref[0, ks, hs], m, l, acc)
        o_ref[0, :, hs] = acc / l


def _moba(q, kT, v, kmean):
    B, S, W = q.shape
    nb = S // MOBA_BLOCK
    nb_lanes = -(-nb // LANES) * LANES
    kmT = jnp.zeros((B, W, nb_lanes), MXU_DTYPE).at[:, :, :nb].set(
        kmean.reshape(B, nb, W).transpose(0, 2, 1).astype(MXU_DTYPE))
    return pl.pallas_call(
        functools.partial(_moba_kernel, nb=nb, nb_lanes=nb_lanes),
        grid=(B, nb),
        in_specs=[pl.BlockSpec((1, MOBA_BLOCK, W), lambda b, i: (b, i, 0)),
                  pl.BlockSpec((1, W, S), lambda b, i: (b, 0, 0)),
                  pl.BlockSpec((1, S, W), lambda b, i: (b, 0, 0)),
                  pl.BlockSpec((1, W, nb_lanes), lambda b, i: (b, 0, 0))],
        out_specs=pl.BlockSpec((1, MOBA_BLOCK, W), lambda b, i: (b, i, 0)),
        out_shape=jax.ShapeDtypeStruct((B, S, W), F32),
        compiler_params=_params(("arbitrary", "arbitrary")),
        name="moba",
    )(q, kT, v, kmT)


def _order_key(x):
    b = pltpu.bitcast(x, I32)
    return b ^ ((b >> 31) & 0x7FFFFFFF)


def _dsa_kernel(q_ref, kT_ref, v_ref, qi_ref, kiT_ref, w_ref, o_ref, sc_ref, *, TQ, TK, RG, topk):
    i = pl.program_id(1)
    n_t = ((i + 1) * TQ + TK - 1) // TK
    trow = i * TQ + lax.broadcasted_iota(I32, (TQ, TK), 0)
    lcol = lax.broadcasted_iota(I32, (TQ, TK), 1)

    def index_tile(j, _):
        ks = pl.ds(pl.multiple_of(j * TK, TK), TK)
        ki = kiT_ref[0, :, ks]
        acc = jnp.zeros((TQ, TK), F32)
        for h in range(IDX_HEADS):
            s = _dot(qi_ref[0, :, h * IDX_DIM:(h + 1) * IDX_DIM], ki)
            acc = acc + w_ref[0, :, h:h + 1] * jnp.maximum(s, 0.0)
        key = jnp.where(j * TK + lcol <= trow, _order_key(acc), INT_MIN)
        sc_ref[:, ks] = key
        return 0
    lax.fori_loop(0, n_t, index_tile, 0)

    lane_g = lax.broadcasted_iota(I32, (RG, LANES), 1)
    for g in range(TQ // RG):
        rows = pl.ds(g * RG, RG)

        def count(pred_fn):
            def tile(j, part):
                base = pl.multiple_of(j * TK, TK)
                for c in range(TK // LANES):
                    blk = sc_ref[rows, pl.ds(base + c * LANES, LANES)]
                    part = part + jnp.where(pred_fn(blk, base + c * LANES + lane_g), 1, 0)
                return part
            part = lax.fori_loop(0, n_t, tile, jnp.zeros((RG, LANES), I32))
            return jnp.sum(part, axis=-1, keepdims=True)

        def bit_step(t, ans_u):
            cand_u = ans_u | lax.shift_left(jnp.int32(1), 31 - t)
            cand = cand_u ^ INT_MIN
            cnt = count(lambda blk, col: blk >= cand)
            return jnp.where(cnt >= topk, cand_u, ans_u)
        ans_u = lax.fori_loop(0, 32, bit_step, jnp.zeros((RG, 1), I32))
        thr = jnp.maximum(ans_u ^ INT_MIN, INT_MIN + 1)
        n_gt = count(lambda blk, col: blk > thr)
        n_eq = count(lambda blk, col: blk == thr)
        need = topk - n_gt
        tied = jnp.logical_and(n_eq > need, need > 0)
        any_tied = jnp.max(jnp.where(tied, 1, 0)) > 0

        def col_search(_):
            def step(t, ans):
                cand = ans | lax.shift_left(jnp.int32(1), 30 - t)
                cnt = count(lambda blk, col: jnp.logical_and(blk == thr, col < cand))
                return jnp.where(cnt < need, cand, ans)
            return lax.fori_loop(0, 31, step, jnp.zeros((RG, 1), I32))
        jmax = lax.cond(any_tied, col_search, lambda _: jnp.full((RG, 1), 2 ** 31 - 1, I32), 0)
        jmax = jnp.where(tied, jmax, 2 ** 31 - 1)

        def to_bias(j, _):
            base = pl.multiple_of(j * TK, TK)
            for c in range(TK // LANES):
                cs = pl.ds(base + c * LANES, LANES)
                blk = sc_ref[rows, cs]
                col = base + c * LANES + lane_g
                keep = jnp.where(blk > thr, 1, jnp.where(blk == thr, jnp.where(col <= jmax, 1, 0), 0))
                sc_ref[rows, cs] = pltpu.bitcast(jnp.where(keep > 0, 0.0, NEG).astype(F32), I32)
            return 0
        lax.fori_loop(0, n_t, to_bias, 0)

    for h in range(DSA_HEADS):
        hs = slice(h * HEAD_DIM, (h + 1) * HEAD_DIM)
        q = q_ref[0, :, hs]

        def att_tile(j, carry):
            m, l, acc = carry
            ks = pl.ds(pl.multiple_of(j * TK, TK), TK)
            s = _dot(q, kT_ref[0, hs, ks]) + pltpu.bitcast(sc_ref[:, ks], F32)
            return _softmax_step(s, v_ref[0, ks, hs], m, l, acc)

        init = (jnp.full((TQ, 1), NEG, F32), jnp.zeros((TQ, 1), F32), jnp.zeros((TQ, HEAD_DIM), F32))
        m, l, acc = lax.fori_loop(0, n_t, att_tile, init)
        o_ref[0, :, hs] = acc / l


def _dsa(q, kT, v, qi, kiT, wi):
    B, S, W = q.shape
    topk = min(DSA_TOPK, S // 4)
    TQ, TK, RG = 256, 512, 64
    TK = min(TK, S)
    return pl.pallas_call(
        functools.partial(_dsa_kernel, TQ=TQ, TK=TK, RG=RG, topk=topk),
        grid=(B, S // TQ),
        in_specs=[pl.BlockSpec((1, TQ, W), lambda b, i: (b, i, 0)),
                  pl.BlockSpec((1, W, S), lambda b, i: (b, 0, 0)),
                  pl.BlockSpec((1, S, W), lambda b, i: (b, 0, 0)),
                  pl.BlockSpec((1, TQ, IDX_HEADS * IDX_DIM), lambda b, i: (b, i, 0)),
                  pl.BlockSpec((1, IDX_DIM, S), lambda b, i: (b, 0, 0)),
                  pl.BlockSpec((1, TQ, IDX_HEADS), lambda b, i: (b, i, 0))],
        out_specs=pl.BlockSpec((1, TQ, W), lambda b, i: (b, i, 0)),
        out_shape=jax.ShapeDtypeStruct((B, S, W), F32),
        scratch_shapes=[pltpu.VMEM((TQ, S), I32)],
        compiler_params=_params(("arbitrary", "arbitrary")),
        name="dsa",
    )(q, kT, v, qi, kiT, wi)


def _mix_kernel(x_ref, ym_ref, yd_ref, u_ref, up_ref, cw_ref, cb_ref, lg_ref, lb_ref, mg_ref, dg_ref,
                wom_ref, woc_ref, wod_ref, g1_ref, n2_ref, sc_ref, sh_ref, rgw_ref, rgb_ref, rew_ref, reb_ref,
                x1_ref, h2_ref, eid_ref, gate_ref, win_ref, *, tm, halo):
    i = pl.program_id(1)
    win_ref[0:halo, :] = jnp.where(i > 0, up_ref[0], 0.0)
    win_ref[halo:halo + tm, :] = u_ref[0]
    rc = 128
    off = halo - (CONV_WIDTH - 1)
    cw = cw_ref[...]
    for r0 in range(0, tm, rc):
        acc = jnp.zeros((rc, CONV_CH), F32) + cb_ref[...]
        for k in range(CONV_WIDTH):
            acc = acc + cw[k:k + 1, :] * win_ref[r0 + off + k:r0 + off + k + rc, :]
        mu = jnp.mean(acc, axis=-1, keepdims=True)
        xc = acc - mu
        yn = xc * lax.rsqrt(jnp.mean(xc * xc, axis=-1, keepdims=True) + EPS) * lg_ref[...] + lb_ref[...]
        win_ref[halo + tm + r0:halo + tm + r0 + rc, :] = yn * jax.nn.sigmoid(yn)
    yc = win_ref[halo + tm:halo + 2 * tm, :]

    ymn = (_rms(ym_ref[0]) * mg_ref[...]).astype(MXU_DTYPE)
    ydn = (_rms(yd_ref[0]) * dg_ref[...]).astype(MXU_DTYPE)
    proj = _dot(ymn, wom_ref[...]) + _dot(yc.astype(MXU_DTYPE), woc_ref[...]) + _dot(ydn, wod_ref[...])
    x1 = x_ref[0] + g1_ref[0] * proj
    x1_ref[0] = x1
    h2 = _rms(x1) * n2_ref[...] * (1.0 + sc_ref[0]) + sh_ref[0]
    h2_ref[0] = h2

    lane = lax.broadcasted_iota(I32, (tm, LANES), 1)
    gl = _dot(h2, rgw_ref[...]) + rgb_ref[...]
    gl = jnp.where(lane < N_GROUPS, gl, -jnp.inf)
    gmax = jnp.max(gl, axis=-1, keepdims=True)
    g_idx = jnp.min(jnp.where(gl == gmax, lane, LANES), axis=-1, keepdims=True)
    g_w = 1.0 / jnp.sum(jnp.exp(gl - gmax), axis=-1, keepdims=True)
    el = _dot(h2, rew_ref[...]) + reb_ref[...]
    in_group = jnp.logical_and(lane >= g_idx * EXPERTS_PER_GROUP, lane < (g_idx + 1) * EXPERTS_PER_GROUP)
    el = jnp.where(in_group, el, -jnp.inf)
    t1 = jnp.max(el, axis=-1, keepdims=True)
    e1 = jnp.min(jnp.where(el == t1, lane, LANES), axis=-1, keepdims=True)
    el2 = jnp.where(lane == e1, -jnp.inf, el)
    t2 = jnp.max(el2, axis=-1, keepdims=True)
    e2 = jnp.min(jnp.where(el2 == t2, lane, LANES), axis=-1, keepdims=True)
    r = jnp.exp(t2 - t1)
    p1 = 1.0 / (1.0 + r)
    lane8 = lax.broadcasted_iota(I32, (tm, 8), 1)
    eid_ref[0] = jnp.where(lane8 == 0, e1, jnp.where(lane8 == 1, e2, 0))
    gate_ref[0] = jnp.where(lane8 == 0, g_w * p1, jnp.where(lane8 == 1, g_w * p1 * r, 0.0))


def _mix(x, ym, yd, u, p, g1, sc2, sh2):
    B, S, D = x.shape
    tm, halo = 512, 32
    row = lambda width: pl.BlockSpec((1, tm, width), lambda b, i: (b, i, 0))
    vec = pl.BlockSpec((1, 1, D), lambda b, i: (b, 0, 0))
    full = lambda a: pl.BlockSpec(a.shape, lambda b, i: (0,) * a.ndim)
    wo = p["w_out"].astype(MXU_DTYPE)
    wom, woc, wod = wo[:ATT_W], wo[ATT_W:ATT_W + CONV_CH], wo[ATT_W + CONV_CH:]
    pad_cols = lambda w: jnp.zeros((w.shape[0], LANES), F32).at[:, :w.shape[1]].set(w)
    pad_row = lambda v: jnp.zeros((1, LANES), F32).at[0, :v.shape[0]].set(v)
    small = [p["conv_w"], p["conv_b"].reshape(1, -1), p["conv_ln_g"].reshape(1, -1),
             p["conv_ln_b"].reshape(1, -1), p["moba_norm_g"].reshape(1, -1), p["dsa_norm_g"].reshape(1, -1),
             wom, woc, wod]
    tail = [p["norm2_g"].reshape(1, D)]
    rt = [pad_cols(p["router_group_w"]), pad_row(p["router_group_b"]),
          pad_cols(p["router_expert_w"]), pad_row(p["router_expert_b"])]
    sds = jax.ShapeDtypeStruct
    return pl.pallas_call(
        functools.partial(_mix_kernel, tm=tm, halo=halo),
        grid=(B, S // tm),
        in_specs=[row(D), row(ATT_W), row(ATT_W), row(CONV_CH),
                  pl.BlockSpec((1, halo, CONV_CH), lambda b, i: (b, jnp.maximum(i * (tm // halo) - 1, 0), 0))]
                 + [full(a) for a in small] + [vec] + [full(a) for a in tail] + [vec, vec] + [full(a) for a in rt],
        out_specs=[row(D), row(D), row(8), row(8)],
        out_shape=[sds((B, S, D), F32), sds((B, S, D), F32), sds((B, S, 8), I32), sds((B, S, 8), F32)],
        scratch_shapes=[pltpu.VMEM((halo + 2 * tm, CONV_CH), F32)],
        compiler_params=_params(("arbitrary", "arbitrary")),
        name="mix_outproj_route",
    )(x, ym, yd, u, u, *small, g1, *tail, sc2, sh2, *rt)


def _ffn_kernel(blk_e_ref, nvalid_ref, rowinfo_ref, h_hbm, gate_ref, w1_ref, w3_ref, w2_ref, y_hbm,
                xbuf, ybuf, gsem, ssem, *, T):
    i = pl.program_id(0)
    nv = nvalid_ref[i]
    base = i * MOE_BLOCK

    @pl.when(nv > 0)
    def _():
        def gather(r, _):
            tok = rowinfo_ref[base + r] >> 1
            pltpu.make_async_copy(h_hbm.at[pl.ds(tok, 1)], xbuf.at[pl.ds(r, 1)], gsem).start()
            return 0
        lax.fori_loop(0, MOE_BLOCK, gather, 0)
        pltpu.make_async_copy(h_hbm.at[pl.ds(0, MOE_BLOCK)], xbuf, gsem).wait()

        xb = xbuf[...].astype(MXU_DTYPE)
        a = _dot(xb, w1_ref[0])
        hid = (a * jax.nn.sigmoid(a)) * _dot(xb, w3_ref[0])
        ybuf[...] = _dot(hid.astype(MXU_DTYPE), w2_ref[0]) * gate_ref[0]

        def scatter(r, _):
            info = rowinfo_ref[base + r]
            dst = (info & 1) * T + (info >> 1)
            pltpu.make_async_copy(ybuf.at[pl.ds(r, 1)], y_hbm.at[pl.ds(dst, 1)], ssem).start()
            return 0
        lax.fori_loop(0, nv, scatter, 0)

        def scatter_wait(r, _):
            pltpu.make_async_copy(ybuf.at[pl.ds(0, 1)], y_hbm.at[pl.ds(0, 1)], ssem).wait()
            return 0
        lax.fori_loop(0, nv, scatter_wait, 0)


def _moe(h2, eid, gates, w1, w3, w2):
    T, D = h2.shape
    n_slots = 2 * T
    P = n_slots + N_EXPERTS * MOE_BLOCK
    nblk = P // MOE_BLOCK
    flat_e = eid.reshape(-1)
    onehot = (flat_e[:, None] == jnp.arange(N_EXPERTS, dtype=I32)[None, :]).astype(I32)
    csum = jnp.cumsum(onehot, axis=0)
    rank = jnp.sum(csum * onehot, axis=1) - 1
    counts = csum[-1]
    padded = (counts + MOE_BLOCK - 1) // MOE_BLOCK * MOE_BLOCK
    pad_end = jnp.cumsum(padded)
    pad_start = pad_end - padded
    dest = pad_start[flat_e] + rank
    rowinfo = jnp.zeros((P,), I32).at[dest].set(jnp.arange(n_slots, dtype=I32))
    rowgate = jnp.zeros((P,), F32).at[dest].set(gates.reshape(-1))
    blk_start = jnp.arange(nblk, dtype=I32) * MOE_BLOCK
    blk_e = jnp.minimum(jnp.searchsorted(pad_end, blk_start, side="right"), N_EXPERTS - 1).astype(I32)
    seg_end = (pad_start + counts)[blk_e]
    nvalid = jnp.clip(seg_end - blk_start, 0, MOE_BLOCK).astype(I32)
    nvalid = jnp.where(blk_start < pad_end[-1], nvalid, 0)

    grid_spec = pltpu.PrefetchScalarGridSpec(
        num_scalar_prefetch=3,
        grid=(nblk,),
        in_specs=[pl.BlockSpec(memory_space=pl.ANY),
                  pl.BlockSpec((1, MOE_BLOCK, 1), lambda i, be, nv, ri: (i, 0, 0)),
                  pl.BlockSpec((1, D, D_EXPERT), lambda i, be, nv, ri: (be[i], 0, 0)),
                  pl.BlockSpec((1, D, D_EXPERT), lambda i, be, nv, ri: (be[i], 0, 0)),
                  pl.BlockSpec((1, D_EXPERT, D), lambda i, be, nv, ri: (be[i], 0, 0))],
        out_specs=pl.BlockSpec(memory_space=pl.ANY),
        scratch_shapes=[pltpu.VMEM((MOE_BLOCK, D), F32), pltpu.VMEM((MOE_BLOCK, D), F32),
                        pltpu.SemaphoreType.DMA, pltpu.SemaphoreType.DMA])
    y = pl.pallas_call(
        functools.partial(_ffn_kernel, T=T),
        grid_spec=grid_spec,
        out_shape=jax.ShapeDtypeStruct((2 * T, D), F32),
        compiler_params=_params(("arbitrary",)),
        name="moe_ffn",
    )(blk_e, nvalid, rowinfo, h2, rowgate.reshape(nblk, MOE_BLOCK, 1), w1, w3, w2)
    return y.reshape(2, T, D)


def _final_kernel(x_ref, y_ref, g2_ref, fg_ref, o_ref):
    x = x_ref[0] + g2_ref[0] * (y_ref[0, 0] + y_ref[1, 0])
    o_ref[0] = _rms(x) * fg_ref[...]


def _final(x1, y, g2, final_g):
    B, S, D = x1.shape
    tm = 512
    return pl.pallas_call(
        _final_kernel,
        grid=(B, S // tm),
        in_specs=[pl.BlockSpec((1, tm, D), lambda b, i: (b, i, 0)),
                  pl.BlockSpec((2, 1, tm, D), lambda b, i: (0, b, i, 0)),
                  pl.BlockSpec((1, 1, D), lambda b, i: (b, 0, 0)),
                  pl.BlockSpec((1, D), lambda b, i: (0, 0))],
        out_specs=pl.BlockSpec((1, tm, D), lambda b, i: (b, i, 0)),
        out_shape=jax.ShapeDtypeStruct((B, S, D), F32),
        compiler_params=_params(("arbitrary", "arbitrary")),
        name="final_norm",
    )(x1, y, g2, final_g.reshape(1, D))


def kernel(x, c, ada_w, ada_b, norm1_g, w_in, conv_w, conv_b, conv_ln_g, conv_ln_b, moba_norm_g, dsa_norm_g, w_out, norm2_g, router_group_w, router_group_b, router_expert_w, router_expert_b, expert_w1, expert_w3, expert_w2, final_g):
    B, S, D = x.shape
    L = ada_w.shape[0]
    mod = _modulation(c, ada_w, ada_b)
    comb = None
    for l in range(L):
        sh1, sc1, g1, sh2, sc2, g2 = [m[:, None, :] for m in jnp.split(mod[l], 6, axis=-1)]
        x, (qm, kmT, vm, kmean, u, qd, kdT, vd, qi, kiT, wi) = _inproj(
            x, comb, norm1_g[l], sc1, sh1, _split_w_in(w_in[l]))
        y_moba = _moba(qm, kmT, vm, kmean)
        y_dsa = _dsa(qd, kdT, vd, qi, kiT, wi)
        p = dict(conv_w=conv_w[l], conv_b=conv_b[l], conv_ln_g=conv_ln_g[l], conv_ln_b=conv_ln_b[l],
                 moba_norm_g=moba_norm_g[l], dsa_norm_g=dsa_norm_g[l], w_out=w_out[l], norm2_g=norm2_g[l],
                 router_group_w=router_group_w[l], router_group_b=router_group_b[l],
                 router_expert_w=router_expert_w[l], router_expert_b=router_expert_b[l])
        x1, h2, eid, gates = _mix(x, y_moba, y_dsa, u, p, g1, sc2, sh2)
        y = _moe(h2.reshape(B * S, D), eid.reshape(B * S, 8)[:, :2], gates.reshape(B * S, 8)[:, :2],
                 expert_w1[l].astype(MXU_DTYPE), expert_w3[l].astype(MXU_DTYPE), expert_w2[l].astype(MXU_DTYPE))
        x = x1
        comb = (y.reshape(2, B, S, D), g2)
    return _final(x, comb[0], comb[1], final_g)
```

```python
import functools

import jax
import jax.numpy as jnp
from jax import lax
from jax.experimental import pallas as pl
from jax.experimental.pallas import tpu as pltpu

F32 = jnp.float32
I32 = jnp.int32
I16 = jnp.int16
MXU_DTYPE = jnp.bfloat16

HEAD_DIM = 64
MOBA_HEADS = 6
MOBA_BLOCK = 256
MOBA_TOPK = 3
CONV_CH = 256
CONV_WIDTH = 31
DSA_HEADS = 6
DSA_TOPK = 256
IDX_HEADS = 8
IDX_DIM = 64
ATT_W = 384
N_GROUPS = 4
EXPERTS_PER_GROUP = 8
N_EXPERTS = 32
D_EXPERT = 512
MOE_BLOCK = 256
EPS = 1e-6

LANES = 128
PACK_ROWS = 16
PAIR_W = 2 * HEAD_DIM
NEG = -1e30
INT_MIN = -2 ** 31
VMEM_LIMIT = 56 * 1024 * 1024
Q_SCALE = HEAD_DIM ** -0.5
IDX_SCALE = IDX_HEADS ** -0.5 * IDX_DIM ** -0.5


def _params(sem):
    return pltpu.CompilerParams(dimension_semantics=sem, vmem_limit_bytes=VMEM_LIMIT)


def _rms(xf):
    return xf * lax.rsqrt(jnp.mean(xf * xf, axis=-1, keepdims=True) + EPS)


def _dot(a, b):
    return jnp.dot(a, b, preferred_element_type=F32)


def _dot_nt(a, b):
    return lax.dot_general(a, b, (((1,), (1,)), ((), ())), preferred_element_type=F32)


def _resident(block_shape, index_map):
    return pl.BlockSpec(block_shape, index_map, pipeline_mode=pl.Buffered(1))


def _mod_kernel(c_ref, w_ref, b_ref, o_ref):
    c = c_ref[...]
    o_ref[0] = _dot(c * jax.nn.sigmoid(c), w_ref[0]) + b_ref[0]


def _modulation(c, ada_w, ada_b):
    L, D, N = ada_w.shape
    B = c.shape[0]
    rows = 8
    cp = jnp.zeros((rows, D), F32).at[:B].set(c)
    tn = 1536
    out = pl.pallas_call(
        _mod_kernel,
        grid=(L, N // tn),
        in_specs=[pl.BlockSpec((rows, D), lambda l, j: (0, 0)),
                  pl.BlockSpec((1, D, tn), lambda l, j: (l, 0, j)),
                  pl.BlockSpec((1, 1, tn), lambda l, j: (l, 0, j))],
        out_specs=pl.BlockSpec((1, rows, tn), lambda l, j: (l, 0, j)),
        out_shape=jax.ShapeDtypeStruct((L, rows, N), F32),
        compiler_params=_params(("arbitrary", "arbitrary")),
        name="adaln_mod",
    )(cp, ada_w, ada_b.reshape(L, 1, N))
    return out[:, :B]


def _inproj_kernel(*refs, combine, tm):
    if combine:
        x_ref, y_ref, g2_ref = refs[:3]
        refs = refs[3:]
    else:
        x_ref = refs[0]
        refs = refs[1:]
    (g_ref, sc_ref, sh_ref, wqmT, wkm, wvmT, wglu, wqdT, wkd, wvdT, wqiT, wki, wwiT) = refs[:13]
    outs = refs[13:]
    if combine:
        xo_ref = outs[0]
        outs = outs[1:]
    (qmT_ref, km_ref, vmT_ref, kmean_ref, u_ref, qdT_ref, kd_ref, vdT_ref,
     qiT_ref, ki_ref, wiT_ref) = outs

    x = x_ref[0]
    if combine:
        x = x + g2_ref[0] * (y_ref[0, 0] + y_ref[1, 0])
        xo_ref[0] = x
    h = _rms(x) * g_ref[...] * (1.0 + sc_ref[0]) + sh_ref[0]
    hb = h.astype(MXU_DTYPE)

    qmT_ref[0] = _dot_nt(wqmT[...], hb).astype(qmT_ref.dtype)
    k_rows = _dot(hb, wkm[...])
    km_ref[0] = k_rows.astype(km_ref.dtype)
    for c in range(tm // MOBA_BLOCK):
        kmean_ref[0, c] = jnp.mean(k_rows[c * MOBA_BLOCK:(c + 1) * MOBA_BLOCK], axis=0, keepdims=True)
    vmT_ref[0] = _dot_nt(wvmT[...], hb).astype(vmT_ref.dtype)
    glu = _dot(hb, wglu[...])
    u_ref[0] = glu[:, :CONV_CH] * jax.nn.sigmoid(glu[:, CONV_CH:])
    qdT_ref[0] = _dot_nt(wqdT[...], hb).astype(qdT_ref.dtype)
    kd_ref[0] = _dot(hb, wkd[...]).astype(kd_ref.dtype)
    vdT_ref[0] = _dot_nt(wvdT[...], hb).astype(vdT_ref.dtype)
    qiT_ref[0] = _dot_nt(wqiT[...], hb).astype(qiT_ref.dtype)
    ki_ref[0] = _dot(hb, wki[...]).astype(ki_ref.dtype)
    wiT_ref[0] = _dot_nt(wwiT[...], hb) * IDX_SCALE


_W_NAMES = ["wqmT", "wkm", "wvmT", "wglu", "wqdT", "wkd", "wvdT", "wqiT", "wki", "wwiT"]


def _split_w_in(w_in):
    offs = [0, 384, 768, 1152, 1408, 1664, 2048, 2432, 2816, 3328, 3392, 3400]
    piece = lambda k: w_in[:, offs[k]:offs[k + 1]]
    c = lambda w: w.astype(MXU_DTYPE)
    return dict(
        wqmT=c(piece(0).T * Q_SCALE), wkm=c(piece(1)), wvmT=c(piece(2).T),
        wglu=c(w_in[:, offs[3]:offs[5]]),
        wqdT=c(piece(5).T * Q_SCALE), wkd=c(piece(6)), wvdT=c(piece(7).T),
        wqiT=c(piece(8).T), wki=c(piece(9)), wwiT=c(piece(10).T))


def _inproj(x, comb, n1g, sc1, sh1, w):
    B, S, D = x.shape
    tm = 512
    nb_t = tm // MOBA_BLOCK
    combine = comb is not None
    row = lambda width: pl.BlockSpec((1, tm, width), lambda b, i: (b, i, 0))
    col = lambda height: pl.BlockSpec((1, height, tm), lambda b, i: (b, 0, i))
    vec = pl.BlockSpec((1, 1, D), lambda b, i: (b, 0, 0))
    full = lambda a: pl.BlockSpec(a.shape, lambda b, i: (0,) * a.ndim)

    ins, in_specs = [x], [row(D)]
    if combine:
        y, g2 = comb
        ins += [y, g2]
        in_specs += [pl.BlockSpec((2, 1, tm, D), lambda b, i: (0, b, i, 0)), vec]
    ins += [n1g.reshape(1, D), sc1, sh1] + [w[k] for k in _W_NAMES]
    in_specs += [pl.BlockSpec((1, D), lambda b, i: (0, 0)), vec, vec] + [full(w[k]) for k in _W_NAMES]

    sds = jax.ShapeDtypeStruct
    QI_W = IDX_HEADS * IDX_DIM
    out_shape, out_specs = [], []
    if combine:
        out_shape.append(sds((B, S, D), F32))
        out_specs.append(row(D))
    out_shape += [
        sds((B, ATT_W, S), MXU_DTYPE), sds((B, S, ATT_W), MXU_DTYPE), sds((B, ATT_W, S), MXU_DTYPE),
        sds((B, S // MOBA_BLOCK, 1, ATT_W), F32),
        sds((B, S, CONV_CH), F32),
        sds((B, ATT_W, S), MXU_DTYPE), sds((B, S, ATT_W), MXU_DTYPE), sds((B, ATT_W, S), MXU_DTYPE),
        sds((B, QI_W, S), MXU_DTYPE), sds((B, S, IDX_DIM), MXU_DTYPE), sds((B, IDX_HEADS, S), F32)]
    out_specs += [
        col(ATT_W), row(ATT_W), col(ATT_W),
        pl.BlockSpec((1, nb_t, 1, ATT_W), lambda b, i: (b, i, 0, 0)),
        row(CONV_CH),
        col(ATT_W), row(ATT_W), col(ATT_W),
        col(QI_W), row(IDX_DIM), col(IDX_HEADS)]

    outs = pl.pallas_call(
        functools.partial(_inproj_kernel, combine=combine, tm=tm),
        grid=(B, S // tm),
        in_specs=in_specs, out_specs=out_specs, out_shape=out_shape,
        compiler_params=_params(("arbitrary", "arbitrary")),
        name="inproj",
    )(*ins)
    if combine:
        return outs[0], outs[1:]
    return x, outs


def _pair_queries(qT_ref, p, TQ):
    first = lax.broadcasted_iota(I32, (PAIR_W, TQ), 0) < HEAD_DIM
    qpair = qT_ref[0, p * PAIR_W:(p + 1) * PAIR_W, :]
    zero = jnp.zeros_like(qpair)
    return jnp.where(first, qpair, zero), jnp.where(first, zero, qpair)


def _flash_step(sT, vT_tile, m, l, acc):
    m_new = jnp.maximum(m, jnp.max(sT, axis=0, keepdims=True))
    alpha = jnp.exp(m - m_new)
    p = jnp.exp(sT - m_new)
    l = alpha * l + jnp.sum(p, axis=0, keepdims=True)
    acc = alpha * acc + _dot(vT_tile, p.astype(vT_tile.dtype))
    return m_new, l, acc


def _flash_init(TQ):
    return (jnp.full((1, TQ), NEG, F32), jnp.zeros((1, TQ), F32), jnp.zeros((HEAD_DIM, TQ), F32))


def _moba_kernel(qT_ref, k_ref, vT_ref, km_ref, o_ref, sb_ref, *, nb):
    TQ = MOBA_BLOCK
    i = pl.program_id(1)
    blk = lax.broadcasted_iota(I32, (nb, TQ), 0)
    past = blk < i
    causal = lax.broadcasted_iota(I32, (TQ, TQ), 0) <= lax.broadcasted_iota(I32, (TQ, TQ), 1)
    own = pl.ds(pl.multiple_of(i * TQ, TQ), TQ)

    for p in range(MOBA_HEADS // 2):
        ps = slice(p * PAIR_W, (p + 1) * PAIR_W)
        qpads = _pair_queries(qT_ref, p, TQ)
        for r in range(2):
            g = jnp.where(past, _dot(km_ref[0, :, ps], qpads[r]), -jnp.inf)
            sel = jnp.zeros((nb, TQ), I32)
            for _ in range(MOBA_TOPK):
                mx = jnp.max(g, axis=0, keepdims=True)
                first = jnp.min(jnp.where(g == mx, blk, nb), axis=0, keepdims=True)
                hit = blk == first
                sel = jnp.where(hit, 1, sel)
                g = jnp.where(hit, -jnp.inf, g)
            sb_ref[r] = jnp.where(past, jnp.where(sel > 0, 0.0, NEG), NEG)

        def past_block(j, carry):
            ks = pl.ds(pl.multiple_of(j * TQ, TQ), TQ)
            kp = k_ref[0, ks, ps]
            out = []
            for r in range(2):
                hrows = slice((2 * p + r) * HEAD_DIM, (2 * p + r + 1) * HEAD_DIM)
                sT = _dot(kp, qpads[r]) + sb_ref[r, pl.ds(j, 1), :]
                out.append(_flash_step(sT, vT_ref[0, hrows, ks], *carry[r]))
            return tuple(out)

        state = lax.fori_loop(0, i, past_block, (_flash_init(TQ), _flash_init(TQ)))
        kp = k_ref[0, own, ps]
        for r in range(2):
            hrows = slice((2 * p + r) * HEAD_DIM, (2 * p + r + 1) * HEAD_DIM)
            sT = jnp.where(causal, _dot(kp, qpads[r]), NEG)
            m, l, acc = _flash_step(sT, vT_ref[0, hrows, own], *state[r])
            o_ref[0, hrows, :] = acc / l


def _moba(qT, k, vT, kmean):
    B, W, S = qT.shape
    nb = S // MOBA_BLOCK
    km = kmean.reshape(B, nb, W).astype(MXU_DTYPE)
    return pl.pallas_call(
        functools.partial(_moba_kernel, nb=nb),
        grid=(B, nb),
        in_specs=[pl.BlockSpec((1, W, MOBA_BLOCK), lambda b, i: (b, 0, i)),
                  _resident((1, S, W), lambda b, i: (b, 0, 0)),
                  _resident((1, W, S), lambda b, i: (b, 0, 0)),
                  _resident((1, nb, W), lambda b, i: (b, 0, 0))],
        out_specs=pl.BlockSpec((1, W, MOBA_BLOCK), lambda b, i: (b, 0, i)),
        out_shape=jax.ShapeDtypeStruct((B, W, S), F32),
        scratch_shapes=[pltpu.VMEM((2, nb, MOBA_BLOCK), F32)],
        compiler_params=_params(("arbitrary", "arbitrary")),
        name="moba",
    )(qT, k, vT, km)


def _order_key(x):
    b = pltpu.bitcast(x, I32)
    return b ^ ((b >> 31) & 0x7FFFFFFF)


def _dsa_kernel(qT_ref, k_ref, vT_ref, qiT_ref, ki_ref, w_ref, o_ref, hi_ref, lo_ref, bias_ref,
                *, TQ, TK, CH, topk):
    i = pl.program_id(1)
    n_t = ((i + 1) * TQ + TK - 1) // TK
    krow = lax.broadcasted_iota(I32, (TK, TQ), 0)
    qcol = i * TQ + lax.broadcasted_iota(I32, (TK, TQ), 1)

    def index_tile(j, _):
        ks = pl.ds(pl.multiple_of(j * TK, TK), TK)
        ki = ki_ref[0, ks, :]
        acc = jnp.zeros((TK, TQ), F32)
        for h in range(IDX_HEADS):
            s = _dot(ki, qiT_ref[0, h * IDX_DIM:(h + 1) * IDX_DIM, :])
            acc = acc + w_ref[0, h:h + 1, :] * jnp.maximum(s, 0.0)
        key = jnp.where(j * TK + krow <= qcol, _order_key(acc), INT_MIN)
        hi_ref[ks, :] = (key >> 16).astype(I16)
        lo_ref[ks, :] = ((key & 0xFFFF) - 32768).astype(I16)
        return 0
    lax.fori_loop(0, n_t, index_tile, 0)

    n_ch = n_t * (TK // CH)
    one, zero = jnp.ones((PACK_ROWS, TQ), I16), jnp.zeros((PACK_ROWS, TQ), I16)
    rows16 = lax.broadcasted_iota(I32, (PACK_ROWS, TQ), 0)

    def rows_to_i16(v):
        return jnp.broadcast_to(v, (PACK_ROWS, TQ)).astype(I16)

    def count(flag):
        def chunk(c, part):
            r0 = pl.multiple_of(c * CH, CH)
            for u in range(CH // PACK_ROWS):
                part = part + flag(pl.ds(r0 + u * PACK_ROWS, PACK_ROWS), r0 + u * PACK_ROWS)
            return part
        part = lax.fori_loop(0, n_ch, chunk, zero)
        return jnp.sum(part.astype(I32), axis=0, keepdims=True)

    def search16(flag_ge, k_needed):
        def step(t, ans_u):
            cand_u = ans_u | lax.shift_left(jnp.int32(1), 15 - t)
            c16 = rows_to_i16(cand_u - 32768)
            cnt = count(lambda rs, r: flag_ge(rs, c16))
            return jnp.where(cnt >= k_needed, cand_u, ans_u)
        return lax.fori_loop(0, 16, step, jnp.zeros((1, TQ), I32)) - 32768

    P = search16(lambda rs, c: jnp.where(hi_ref[rs, :] >= c, one, zero), topk)
    P = jnp.maximum(P, -32767)
    P16 = rows_to_i16(P)
    n_hi_gt = count(lambda rs, r: jnp.where(hi_ref[rs, :] > P16, one, zero))

    def restrict_lo(c, _):
        r0 = pl.multiple_of(c * CH, CH)
        for u in range(CH // PACK_ROWS):
            rs = pl.ds(r0 + u * PACK_ROWS, PACK_ROWS)
            lo_ref[rs, :] = jnp.where(hi_ref[rs, :] == P16, lo_ref[rs, :], jnp.full((PACK_ROWS, TQ), -32768, I16))
        return 0
    lax.fori_loop(0, n_ch, restrict_lo, 0)
    Q = search16(lambda rs, c: jnp.where(lo_ref[rs, :] >= c, one, zero), topk - n_hi_gt)
    Q16 = rows_to_i16(Q)

    def in_bucket_eq(rs):
        return jnp.where(hi_ref[rs, :] == P16, jnp.where(lo_ref[rs, :] == Q16, one, zero), zero)
    n_gt = n_hi_gt + count(lambda rs, r: jnp.where(lo_ref[rs, :] > Q16, one, zero))
    n_eq = count(lambda rs, r: in_bucket_eq(rs))
    need = topk - n_gt
    tied = n_eq > need
    any_tied = jnp.max(jnp.where(tied, 1, 0)) > 0

    def last_tie_row(_):
        def step(t, ans):
            cand = ans | lax.shift_left(jnp.int32(1), 14 - t)
            c16 = rows_to_i16(cand)
            cnt = count(lambda rs, r: jnp.where((r + rows16).astype(I16) < c16, in_bucket_eq(rs), zero))
            return jnp.where(cnt < need, cand, ans)
        return lax.fori_loop(0, 15, step, jnp.zeros((1, TQ), I32))
    jmax = lax.cond(any_tied, last_tie_row, lambda _: jnp.full((1, TQ), 32767, I32), 0)
    J16 = rows_to_i16(jnp.where(tied, jmax, 32767))

    def to_bias(c, _):
        r0 = pl.multiple_of(c * CH, CH)
        for u in range(CH // PACK_ROWS):
            r = r0 + u * PACK_ROWS
            rs = pl.ds(r, PACK_ROWS)
            hi, lo = hi_ref[rs, :], lo_ref[rs, :]
            tie_keep = jnp.where((r + rows16).astype(I16) <= J16, one, zero)
            in_p = jnp.where(lo > Q16, one, jnp.where(lo == Q16, tie_keep, zero))
            keep = jnp.where(hi > P16, one, jnp.where(hi == P16, in_p, zero))
            bias_ref[rs, :] = jnp.where(keep.astype(I32) > 0, 0.0, NEG)
        return 0
    lax.fori_loop(0, n_ch, to_bias, 0)

    for p in range(DSA_HEADS // 2):
        ps = slice(p * PAIR_W, (p + 1) * PAIR_W)
        qpads = _pair_queries(qT_ref, p, TQ)

        def att_tile(j, carry):
            ks = pl.ds(pl.multiple_of(j * TK, TK), TK)
            kp = k_ref[0, ks, ps]
            bias = bias_ref[ks, :]
            out = []
            for r in range(2):
                hrows = slice((2 * p + r) * HEAD_DIM, (2 * p + r + 1) * HEAD_DIM)
                out.append(_flash_step(_dot(kp, qpads[r]) + bias, vT_ref[0, hrows, ks], *carry[r]))
            return tuple(out)

        state = lax.fori_loop(0, n_t, att_tile, (_flash_init(TQ), _flash_init(TQ)))
        for r in range(2):
            hrows = slice((2 * p + r) * HEAD_DIM, (2 * p + r + 1) * HEAD_DIM)
            m, l, acc = state[r]
            o_ref[0, hrows, :] = acc / l


def _dsa(qT, k, vT, qiT, ki, wT):
    B, W, S = qT.shape
    assert S < 2 ** 15, "key indices are compared as int16"
    topk = min(DSA_TOPK, S // 4)
    TQ, TK, CH = 256, min(512, S), 256
    return pl.pallas_call(
        functools.partial(_dsa_kernel, TQ=TQ, TK=TK, CH=CH, topk=topk),
        grid=(B, S // TQ),
        in_specs=[pl.BlockSpec((1, W, TQ), lambda b, i: (b, 0, i)),
                  _resident((1, S, W), lambda b, i: (b, 0, 0)),
                  _resident((1, W, S), lambda b, i: (b, 0, 0)),
                  pl.BlockSpec((1, IDX_HEADS * IDX_DIM, TQ), lambda b, i: (b, 0, i)),
                  _resident((1, S, IDX_DIM), lambda b, i: (b, 0, 0)),
                  pl.BlockSpec((1, IDX_HEADS, TQ), lambda b, i: (b, 0, i))],
        out_specs=pl.BlockSpec((1, W, TQ), lambda b, i: (b, 0, i)),
        out_shape=jax.ShapeDtypeStruct((B, W, S), F32),
        scratch_shapes=[pltpu.VMEM((S, TQ), I16), pltpu.VMEM((S, TQ), I16), pltpu.VMEM((S, TQ), F32)],
        compiler_params=_params(("arbitrary", "arbitrary")),
        name="dsa",
    )(qT, k, vT, qiT, ki, wT)


def _mix_kernel(x_ref, ym_ref, yd_ref, u_ref, up_ref, cw_ref, cb_ref, lg_ref, lb_ref, mg_ref, dg_ref,
                wom_ref, woc_ref, wod_ref, g1_ref, n2_ref, sc_ref, sh_ref, rgw_ref, rgb_ref, rew_ref, reb_ref,
                x1_ref, h2_ref, eid_ref, gate_ref, win_ref, *, tm, halo):
    i = pl.program_id(1)
    win_ref[0:halo, :] = jnp.where(i > 0, up_ref[0], 0.0)
    win_ref[halo:halo + tm, :] = u_ref[0]
    rc = 128
    off = halo - (CONV_WIDTH - 1)
    cw = cw_ref[...]
    for r0 in range(0, tm, rc):
        acc = jnp.zeros((rc, CONV_CH), F32) + cb_ref[...]
        for k in range(CONV_WIDTH):
            acc = acc + cw[k:k + 1, :] * win_ref[r0 + off + k:r0 + off + k + rc, :]
        mu = jnp.mean(acc, axis=-1, keepdims=True)
        xc = acc - mu
        yn = xc * lax.rsqrt(jnp.mean(xc * xc, axis=-1, keepdims=True) + EPS) * lg_ref[...] + lb_ref[...]
        win_ref[halo + tm + r0:halo + tm + r0 + rc, :] = yn * jax.nn.sigmoid(yn)
    yc = win_ref[halo + tm:halo + 2 * tm, :]

    ymn = (_rms(ym_ref[0].T) * mg_ref[...]).astype(MXU_DTYPE)
    ydn = (_rms(yd_ref[0].T) * dg_ref[...]).astype(MXU_DTYPE)
    proj = _dot(ymn, wom_ref[...]) + _dot(yc.astype(MXU_DTYPE), woc_ref[...]) + _dot(ydn, wod_ref[...])
    x1 = x_ref[0] + g1_ref[0] * proj
    x1_ref[0] = x1
    h2 = _rms(x1) * n2_ref[...] * (1.0 + sc_ref[0]) + sh_ref[0]
    h2_ref[0] = h2

    lane = lax.broadcasted_iota(I32, (tm, LANES), 1)
    gl = _dot(h2, rgw_ref[...]) + rgb_ref[...]
    gl = jnp.where(lane < N_GROUPS, gl, -jnp.inf)
    gmax = jnp.max(gl, axis=-1, keepdims=True)
    g_idx = jnp.min(jnp.where(gl == gmax, lane, LANES), axis=-1, keepdims=True)
    g_w = 1.0 / jnp.sum(jnp.exp(gl - gmax), axis=-1, keepdims=True)
    el = _dot(h2, rew_ref[...]) + reb_ref[...]
    in_group = jnp.logical_and(lane >= g_idx * EXPERTS_PER_GROUP, lane < (g_idx + 1) * EXPERTS_PER_GROUP)
    el = jnp.where(in_group, el, -jnp.inf)
    t1 = jnp.max(el, axis=-1, keepdims=True)
    e1 = jnp.min(jnp.where(el == t1, lane, LANES), axis=-1, keepdims=True)
    el2 = jnp.where(lane == e1, -jnp.inf, el)
    t2 = jnp.max(el2, axis=-1, keepdims=True)
    e2 = jnp.min(jnp.where(el2 == t2, lane, LANES), axis=-1, keepdims=True)
    r = jnp.exp(t2 - t1)
    p1 = 1.0 / (1.0 + r)
    lane8 = lax.broadcasted_iota(I32, (tm, 8), 1)
    eid_ref[0] = jnp.where(lane8 == 0, e1, jnp.where(lane8 == 1, e2, 0))
    gate_ref[0] = jnp.where(lane8 == 0, g_w * p1, jnp.where(lane8 == 1, g_w * p1 * r, 0.0))


def _mix(x, ymT, ydT, u, p, g1, sc2, sh2):
    B, S, D = x.shape
    tm, halo = 512, 32
    row = lambda width: pl.BlockSpec((1, tm, width), lambda b, i: (b, i, 0))
    colT = pl.BlockSpec((1, ATT_W, tm), lambda b, i: (b, 0, i))
    vec = pl.BlockSpec((1, 1, D), lambda b, i: (b, 0, 0))
    full = lambda a: pl.BlockSpec(a.shape, lambda b, i: (0,) * a.ndim)
    wo = p["w_out"].astype(MXU_DTYPE)
    wom, woc, wod = wo[:ATT_W], wo[ATT_W:ATT_W + CONV_CH], wo[ATT_W + CONV_CH:]
    pad_cols = lambda w: jnp.zeros((w.shape[0], LANES), F32).at[:, :w.shape[1]].set(w)
    pad_row = lambda v: jnp.zeros((1, LANES), F32).at[0, :v.shape[0]].set(v)
    small = [p["conv_w"], p["conv_b"].reshape(1, -1), p["conv_ln_g"].reshape(1, -1),
             p["conv_ln_b"].reshape(1, -1), p["moba_norm_g"].reshape(1, -1), p["dsa_norm_g"].reshape(1, -1),
             wom, woc, wod]
    tail = [p["norm2_g"].reshape(1, D)]
    rt = [pad_cols(p["router_group_w"]), pad_row(p["router_group_b"]),
          pad_cols(p["router_expert_w"]), pad_row(p["router_expert_b"])]
    sds = jax.ShapeDtypeStruct
    return pl.pallas_call(
        functools.partial(_mix_kernel, tm=tm, halo=halo),
        grid=(B, S // tm),
        in_specs=[row(D), colT, colT, row(CONV_CH),
                  pl.BlockSpec((1, halo, CONV_CH), lambda b, i: (b, jnp.maximum(i * (tm // halo) - 1, 0), 0))]
                 + [full(a) for a in small] + [vec] + [full(a) for a in tail] + [vec, vec] + [full(a) for a in rt],
        out_specs=[row(D), row(D), row(8), row(8)],
        out_shape=[sds((B, S, D), F32), sds((B, S, D), F32), sds((B, S, 8), I32), sds((B, S, 8), F32)],
        scratch_shapes=[pltpu.VMEM((halo + 2 * tm, CONV_CH), F32)],
        compiler_params=_params(("arbitrary", "arbitrary")),
        name="mix_outproj_route",
    )(x, ymT, ydT, u, u, *small, g1, *tail, sc2, sh2, *rt)


def _ffn_kernel(blk_e_ref, nvalid_ref, rowinfo_ref, h_hbm, gate_ref, w1_ref, w3_ref, w2_ref, y_hbm,
                xbuf, ybuf, gsem, ssem, *, T):
    i = pl.program_id(0)
    nv = nvalid_ref[i]
    base = i * MOE_BLOCK

    @pl.when(nv > 0)
    def _():
        def gather(r, _):
            tok = rowinfo_ref[base + r] >> 1
            pltpu.make_async_copy(h_hbm.at[pl.ds(tok, 1)], xbuf.at[pl.ds(r, 1)], gsem).start()
            return 0
        lax.fori_loop(0, MOE_BLOCK, gather, 0)
        pltpu.make_async_copy(h_hbm.at[pl.ds(0, MOE_BLOCK)], xbuf, gsem).wait()

        xb = xbuf[...].astype(MXU_DTYPE)
        a = _dot(xb, w1_ref[0])
        hid = (a * jax.nn.sigmoid(a)) * _dot(xb, w3_ref[0])
        ybuf[...] = _dot(hid.astype(MXU_DTYPE), w2_ref[0]) * gate_ref[0]

        def scatter(r, _):
            info = rowinfo_ref[base + r]
            dst = (info & 1) * T + (info >> 1)
            pltpu.make_async_copy(ybuf.at[pl.ds(r, 1)], y_hbm.at[pl.ds(dst, 1)], ssem).start()
            return 0
        lax.fori_loop(0, nv, scatter, 0)

        def scatter_wait(r, _):
            pltpu.make_async_copy(ybuf.at[pl.ds(0, 1)], y_hbm.at[pl.ds(0, 1)], ssem).wait()
            return 0
        lax.fori_loop(0, nv, scatter_wait, 0)


def _moe(h2, eid, gates, w1, w3, w2):
    T, D = h2.shape
    n_slots = 2 * T
    P = n_slots + N_EXPERTS * MOE_BLOCK
    nblk = P // MOE_BLOCK
    flat_e = eid.reshape(-1)
    onehot = (flat_e[:, None] == jnp.arange(N_EXPERTS, dtype=I32)[None, :]).astype(I32)
    csum = jnp.cumsum(onehot, axis=0)
    rank = jnp.sum(csum * onehot, axis=1) - 1
    counts = csum[-1]
    padded = (counts + MOE_BLOCK - 1) // MOE_BLOCK * MOE_BLOCK
    pad_end = jnp.cumsum(padded)
    pad_start = pad_end - padded
    dest = pad_start[flat_e] + rank
    rowinfo = jnp.zeros((P,), I32).at[dest].set(jnp.arange(n_slots, dtype=I32))
    rowgate = jnp.zeros((P,), F32).at[dest].set(gates.reshape(-1))
    blk_start = jnp.arange(nblk, dtype=I32) * MOE_BLOCK
    blk_e = jnp.minimum(jnp.searchsorted(pad_end, blk_start, side="right"), N_EXPERTS - 1).astype(I32)
    seg_end = (pad_start + counts)[blk_e]
    nvalid = jnp.clip(seg_end - blk_start, 0, MOE_BLOCK).astype(I32)
    nvalid = jnp.where(blk_start < pad_end[-1], nvalid, 0)

    grid_spec = pltpu.PrefetchScalarGridSpec(
        num_scalar_prefetch=3,
        grid=(nblk,),
        in_specs=[pl.BlockSpec(memory_space=pl.ANY),
                  pl.BlockSpec((1, MOE_BLOCK, 1), lambda i, be, nv, ri: (i, 0, 0)),
                  pl.BlockSpec((1, D, D_EXPERT), lambda i, be, nv, ri: (be[i], 0, 0)),
                  pl.BlockSpec((1, D, D_EXPERT), lambda i, be, nv, ri: (be[i], 0, 0)),
                  pl.BlockSpec((1, D_EXPERT, D), lambda i, be, nv, ri: (be[i], 0, 0))],
        out_specs=pl.BlockSpec(memory_space=pl.ANY),
        scratch_shapes=[pltpu.VMEM((MOE_BLOCK, D), F32), pltpu.VMEM((MOE_BLOCK, D), F32),
                        pltpu.SemaphoreType.DMA, pltpu.SemaphoreType.DMA])
    y = pl.pallas_call(
        functools.partial(_ffn_kernel, T=T),
        grid_spec=grid_spec,
        out_shape=jax.ShapeDtypeStruct((2 * T, D), F32),
        compiler_params=_params(("arbitrary",)),
        name="moe_ffn",
    )(blk_e, nvalid, rowinfo, h2, rowgate.reshape(nblk, MOE_BLOCK, 1), w1, w3, w2)
    return y.reshape(2, T, D)


def _final_kernel(x_ref, y_ref, g2_ref, fg_ref, o_ref):
    x = x_ref[0] + g2_ref[0] * (y_ref[0, 0] + y_ref[1, 0])
    o_ref[0] = _rms(x) * fg_ref[...]


def _final(x1, y, g2, final_g):
    B, S, D = x1.shape
    tm = 512
    return pl.pallas_call(
        _final_kernel,
        grid=(B, S // tm),
        in_specs=[pl.BlockSpec((1, tm, D), lambda b, i: (b, i, 0)),
                  pl.BlockSpec((2, 1, tm, D), lambda b, i: (0, b, i, 0)),
                  pl.BlockSpec((1, 1, D), lambda b, i: (b, 0, 0)),
                  pl.BlockSpec((1, D), lambda b, i: (0, 0))],
        out_specs=pl.BlockSpec((1, tm, D), lambda b, i: (b, i, 0)),
        out_shape=jax.ShapeDtypeStruct((B, S, D), F32),
        compiler_params=_params(("arbitrary", "arbitrary")),
        name="final_norm",
    )(x1, y, g2, final_g.reshape(1, D))


def kernel(x, c, ada_w, ada_b, norm1_g, w_in, conv_w, conv_b, conv_ln_g, conv_ln_b, moba_norm_g, dsa_norm_g, w_out, norm2_g, router_group_w, router_group_b, router_expert_w, router_expert_b, expert_w1, expert_w3, expert_w2, final_g):
    B, S, D = x.shape
    L = ada_w.shape[0]
    mod = _modulation(c, ada_w, ada_b)
    comb = None
    for l in range(L):
        sh1, sc1, g1, sh2, sc2, g2 = [m[:, None, :] for m in jnp.split(mod[l], 6, axis=-1)]
        x, (qmT, km, vmT, kmean, u, qdT, kd, vdT, qiT, ki, wT) = _inproj(
            x, comb, norm1_g[l], sc1, sh1, _split_w_in(w_in[l]))
        y_moba = _moba(qmT, km, vmT, kmean)
        y_dsa = _dsa(qdT, kd, vdT, qiT, ki, wT)
        p = dict(conv_w=conv_w[l], conv_b=conv_b[l], conv_ln_g=conv_ln_g[l], conv_ln_b=conv_ln_b[l],
                 moba_norm_g=moba_norm_g[l], dsa_norm_g=dsa_norm_g[l], w_out=w_out[l], norm2_g=norm2_g[l],
                 router_group_w=router_group_w[l], router_group_b=router_group_b[l],
                 router_expert_w=router_expert_w[l], router_expert_b=router_expert_b[l])
        x1, h2, eid, gates = _mix(x, y_moba, y_dsa, u, p, g1, sc2, sh2)
        y = _moe(h2.reshape(B * S, D), eid.reshape(B * S, 8)[:, :2], gates.reshape(B * S, 8)[:, :2],
                 expert_w1[l].astype(MXU_DTYPE), expert_w3[l].astype(MXU_DTYPE), expert_w2[l].astype(MXU_DTYPE))
        x = x1
        comb = (y.reshape(2, B, S, D), g2)
    return _final(x, comb[0], comb[1], final_g)
```

```python
import functools

import jax
import jax.numpy as jnp
from jax import lax
from jax.experimental import pallas as pl
from jax.experimental.pallas import tpu as pltpu

F32 = jnp.float32
I32 = jnp.int32
I16 = jnp.int16
MXU_DTYPE = jnp.bfloat16

HEAD_DIM = 64
MOBA_HEADS = 6
MOBA_BLOCK = 256
MOBA_TOPK = 3
CONV_CH = 256
CONV_WIDTH = 31
DSA_HEADS = 6
DSA_TOPK = 256
IDX_HEADS = 8
IDX_DIM = 64
ATT_W = 384
N_GROUPS = 4
EXPERTS_PER_GROUP = 8
N_EXPERTS = 32
D_EXPERT = 512
MOE_BLOCK = 256
EPS = 1e-6

LANES = 128
PACK_ROWS = 16
PAIR_W = 2 * HEAD_DIM
NEG = -1e30
INT_MIN = -2 ** 31
VMEM_LIMIT = 56 * 1024 * 1024
Q_SCALE = HEAD_DIM ** -0.5
IDX_SCALE = IDX_HEADS ** -0.5 * IDX_DIM ** -0.5


def _params(sem):
    return pltpu.CompilerParams(dimension_semantics=sem, vmem_limit_bytes=VMEM_LIMIT)


def _rms(xf):
    return xf * lax.rsqrt(jnp.mean(xf * xf, axis=-1, keepdims=True) + EPS)


def _dot(a, b):
    return jnp.dot(a, b, preferred_element_type=F32)


def _dot_nt(a, b):
    return lax.dot_general(a, b, (((1,), (1,)), ((), ())), preferred_element_type=F32)


def _resident(block_shape, index_map):
    return pl.BlockSpec(block_shape, index_map, pipeline_mode=pl.Buffered(1))


def _mod_kernel(c_ref, w_ref, b_ref, o_ref):
    c = c_ref[...]
    o_ref[0] = _dot(c * jax.nn.sigmoid(c), w_ref[0]) + b_ref[0]


def _modulation(c, ada_w, ada_b):
    L, D, N = ada_w.shape
    B = c.shape[0]
    rows = 8
    cp = jnp.zeros((rows, D), F32).at[:B].set(c)
    tn = 1536
    out = pl.pallas_call(
        _mod_kernel,
        grid=(L, N // tn),
        in_specs=[pl.BlockSpec((rows, D), lambda l, j: (0, 0)),
                  pl.BlockSpec((1, D, tn), lambda l, j: (l, 0, j)),
                  pl.BlockSpec((1, 1, tn), lambda l, j: (l, 0, j))],
        out_specs=pl.BlockSpec((1, rows, tn), lambda l, j: (l, 0, j)),
        out_shape=jax.ShapeDtypeStruct((L, rows, N), F32),
        compiler_params=_params(("arbitrary", "arbitrary")),
        name="adaln_mod",
    )(cp, ada_w, ada_b.reshape(L, 1, N))
    return out[:, :B]


def _inproj_kernel(*refs, combine, tm):
    if combine:
        x_ref, y_ref, g2_ref = refs[:3]
        refs = refs[3:]
    else:
        x_ref = refs[0]
        refs = refs[1:]
    (g_ref, sc_ref, sh_ref, wqmT, wkm, wvmT, wglu, wqdT, wkd, wvdT, wqiT, wki, wwiT) = refs[:13]
    outs = refs[13:]
    if combine:
        xo_ref = outs[0]
        outs = outs[1:]
    (qmT_ref, km_ref, vmT_ref, kmean_ref, u_ref, qdT_ref, kd_ref, vdT_ref,
     qiT_ref, ki_ref, wiT_ref) = outs

    x = x_ref[0]
    if combine:
        x = x + g2_ref[0] * (y_ref[0, 0] + y_ref[1, 0])
        xo_ref[0] = x
    h = _rms(x) * g_ref[...] * (1.0 + sc_ref[0]) + sh_ref[0]
    hb = h.astype(MXU_DTYPE)

    qmT_ref[0] = _dot_nt(wqmT[...], hb).astype(qmT_ref.dtype)
    k_rows = _dot(hb, wkm[...])
    km_ref[0] = k_rows.astype(km_ref.dtype)
    for c in range(tm // MOBA_BLOCK):
        kmean_ref[0, c] = jnp.mean(k_rows[c * MOBA_BLOCK:(c + 1) * MOBA_BLOCK], axis=0, keepdims=True)
    vmT_ref[0] = _dot_nt(wvmT[...], hb).astype(vmT_ref.dtype)
    glu = _dot(hb, wglu[...])
    u_ref[0] = glu[:, :CONV_CH] * jax.nn.sigmoid(glu[:, CONV_CH:])
    qdT_ref[0] = _dot_nt(wqdT[...], hb).astype(qdT_ref.dtype)
    kd_ref[0] = _dot(hb, wkd[...]).astype(kd_ref.dtype)
    vdT_ref[0] = _dot_nt(wvdT[...], hb).astype(vdT_ref.dtype)
    qiT_ref[0] = _dot_nt(wqiT[...], hb).astype(qiT_ref.dtype)
    ki_ref[0] = _dot(hb, wki[...]).astype(ki_ref.dtype)
    wiT_ref[0] = _dot_nt(wwiT[...], hb) * IDX_SCALE


_W_NAMES = ["wqmT", "wkm", "wvmT", "wglu", "wqdT", "wkd", "wvdT", "wqiT", "wki", "wwiT"]


def _split_w_in(w_in):
    offs = [0, 384, 768, 1152, 1408, 1664, 2048, 2432, 2816, 3328, 3392, 3400]
    piece = lambda k: w_in[:, offs[k]:offs[k + 1]]
    c = lambda w: w.astype(MXU_DTYPE)
    return dict(
        wqmT=c(piece(0).T * Q_SCALE), wkm=c(piece(1)), wvmT=c(piece(2).T),
        wglu=c(w_in[:, offs[3]:offs[5]]),
        wqdT=c(piece(5).T * Q_SCALE), wkd=c(piece(6)), wvdT=c(piece(7).T),
        wqiT=c(piece(8).T), wki=c(piece(9)), wwiT=c(piece(10).T))


def _inproj(x, comb, n1g, sc1, sh1, w):
    B, S, D = x.shape
    tm = 512
    nb_t = tm // MOBA_BLOCK
    combine = comb is not None
    row = lambda width: pl.BlockSpec((1, tm, width), lambda b, i: (b, i, 0))
    col = lambda height: pl.BlockSpec((1, height, tm), lambda b, i: (b, 0, i))
    vec = pl.BlockSpec((1, 1, D), lambda b, i: (b, 0, 0))
    full = lambda a: pl.BlockSpec(a.shape, lambda b, i: (0,) * a.ndim)

    ins, in_specs = [x], [row(D)]
    if combine:
        y, g2 = comb
        ins += [y, g2]
        in_specs += [pl.BlockSpec((2, 1, tm, D), lambda b, i: (0, b, i, 0)), vec]
    ins += [n1g.reshape(1, D), sc1, sh1] + [w[k] for k in _W_NAMES]
    in_specs += [pl.BlockSpec((1, D), lambda b, i: (0, 0)), vec, vec] + [full(w[k]) for k in _W_NAMES]

    sds = jax.ShapeDtypeStruct
    QI_W = IDX_HEADS * IDX_DIM
    out_shape, out_specs = [], []
    if combine:
        out_shape.append(sds((B, S, D), F32))
        out_specs.append(row(D))
    out_shape += [
        sds((B, ATT_W, S), MXU_DTYPE), sds((B, S, ATT_W), MXU_DTYPE), sds((B, ATT_W, S), MXU_DTYPE),
        sds((B, S // MOBA_BLOCK, 1, ATT_W), F32),
        sds((B, S, CONV_CH), F32),
        sds((B, ATT_W, S), MXU_DTYPE), sds((B, S, ATT_W), MXU_DTYPE), sds((B, ATT_W, S), MXU_DTYPE),
        sds((B, QI_W, S), MXU_DTYPE), sds((B, S, IDX_DIM), MXU_DTYPE), sds((B, IDX_HEADS, S), F32)]
    out_specs += [
        col(ATT_W), row(ATT_W), col(ATT_W),
        pl.BlockSpec((1, nb_t, 1, ATT_W), lambda b, i: (b, i, 0, 0)),
        row(CONV_CH),
        col(ATT_W), row(ATT_W), col(ATT_W),
        col(QI_W), row(IDX_DIM), col(IDX_HEADS)]

    outs = pl.pallas_call(
        functools.partial(_inproj_kernel, combine=combine, tm=tm),
        grid=(B, S // tm),
        in_specs=in_specs, out_specs=out_specs, out_shape=out_shape,
        compiler_params=_params(("arbitrary", "arbitrary")),
        name="inproj",
    )(*ins)
    if combine:
        return outs[0], outs[1:]
    return x, outs


def _pair_queries(qT_ref, p, TQ):
    first = lax.broadcasted_iota(I32, (PAIR_W, TQ), 0) < HEAD_DIM
    qpair = qT_ref[0, p * PAIR_W:(p + 1) * PAIR_W, :]
    zero = jnp.zeros_like(qpair)
    return jnp.where(first, qpair, zero), jnp.where(first, zero, qpair)


def _fold8(x, op):
    return op(x.reshape(x.shape[0] // 8, 8, x.shape[1]), axis=0)


def _pair_attention(k_ref, vT_ref, ps, hrows, qpads, n_tiles, bias_fn, s_ref, p_ref, *, TK, RC, TQ):
    n_chunks = TK // RC

    def scores(ks, c, r):
        rows = pl.ds(ks + c * RC, RC)
        return _dot(k_ref[0, rows, ps], qpads[r]) + bias_fn(r, ks, c)

    def section(ks_next, slot_next, ks_cur, slot_cur, m_new, m_old, l, acc):
        runmax = jnp.full((8, TQ), NEG, F32)
        lsum = jnp.zeros((8, TQ), F32)
        for c in range(n_chunks):
            cs = slice(c * RC, (c + 1) * RC)
            s = scores(ks_next, c, slot_next)
            s_ref[slot_next, cs, :] = s
            runmax = jnp.maximum(runmax, _fold8(s, jnp.max))
            pc = jnp.exp(s_ref[slot_cur, cs, :] - m_new)
            lsum = lsum + _fold8(pc, jnp.sum)
            p_ref[cs, :] = pc.astype(p_ref.dtype)
        alpha = jnp.exp(m_old - m_new)
        l = alpha * l + jnp.sum(lsum, axis=0, keepdims=True)
        acc = alpha * acc + _dot(vT_ref[0, hrows[slot_cur], pl.ds(ks_cur, TK)], p_ref[...])
        return jnp.max(runmax, axis=0, keepdims=True), l, acc

    runmax = jnp.full((8, TQ), NEG, F32)
    for c in range(n_chunks):
        s = scores(0, c, 0)
        s_ref[0, c * RC:(c + 1) * RC, :] = s
        runmax = jnp.maximum(runmax, _fold8(s, jnp.max))
    tmax0 = jnp.max(runmax, axis=0, keepdims=True)

    def tile(j, carry):
        (m0, l0, a0), (m1, l1, a1), tmax = carry
        ks = pl.multiple_of(j * TK, TK)
        ks_next = pl.multiple_of(jnp.minimum(j + 1, n_tiles - 1) * TK, TK)
        m0n = jnp.maximum(m0, tmax)
        tmax1, l0, a0 = section(ks, 1, ks, 0, m0n, m0, l0, a0)
        m1n = jnp.maximum(m1, tmax1)
        tmax0n, l1, a1 = section(ks_next, 0, ks, 1, m1n, m1, l1, a1)
        return (m0n, l0, a0), (m1n, l1, a1), tmax0n

    init = (jnp.full((1, TQ), NEG, F32), jnp.zeros((1, TQ), F32), jnp.zeros((HEAD_DIM, TQ), F32))
    (_, l0, a0), (_, l1, a1), _ = lax.fori_loop(0, n_tiles, tile, (init, init, tmax0))
    return a0 / l0, a1 / l1


def _moba_kernel(qT_ref, k_ref, vT_ref, km_ref, o_ref, sb_ref, cb_ref, s_ref, p_ref, *, nb, TK, RC):
    TQ = MOBA_BLOCK
    i = pl.program_id(1)
    blk = lax.broadcasted_iota(I32, (nb, TQ), 0)
    past = blk < i
    cb_ref[...] = jnp.where(lax.broadcasted_iota(I32, (TQ, TQ), 0) <= lax.broadcasted_iota(I32, (TQ, TQ), 1),
                            0.0, NEG)
    blocks_per_tile = TK // TQ
    n_tiles = (i + blocks_per_tile) // blocks_per_tile

    for p in range(MOBA_HEADS // 2):
        ps = slice(p * PAIR_W, (p + 1) * PAIR_W)
        hrows = [slice((2 * p + r) * HEAD_DIM, (2 * p + r + 1) * HEAD_DIM) for r in range(2)]
        qpads = _pair_queries(qT_ref, p, TQ)
        for r in range(2):
            g = jnp.where(past, _dot(km_ref[0, :, ps], qpads[r]), -jnp.inf)
            sel = jnp.zeros((nb, TQ), I32)
            for _ in range(MOBA_TOPK):
                mx = jnp.max(g, axis=0, keepdims=True)
                first = jnp.min(jnp.where(g == mx, blk, nb), axis=0, keepdims=True)
                hit = blk == first
                sel = jnp.where(hit, 1, sel)
                g = jnp.where(hit, -jnp.inf, g)
            sb_ref[r] = jnp.where(past, jnp.where(sel > 0, 0.0, NEG), NEG)

        def bias_fn(r, ks, c):
            b = ks // TQ + (c * RC) // TQ
            within = (c * RC) % TQ
            row = jnp.broadcast_to(sb_ref[r, pl.ds(b, 1), :], (RC, TQ))
            return jnp.where(b == i, cb_ref[within:within + RC, :], row)

        outs = _pair_attention(k_ref, vT_ref, ps, hrows, qpads, n_tiles, bias_fn, s_ref, p_ref,
                               TK=TK, RC=RC, TQ=TQ)
        for r in range(2):
            o_ref[0, hrows[r], :] = outs[r]


def _moba(qT, k, vT, kmean):
    B, W, S = qT.shape
    nb = S // MOBA_BLOCK
    TK, RC = min(1024, S), 128
    assert S % TK == 0 and TK % MOBA_BLOCK == 0 and MOBA_BLOCK % RC == 0
    km = kmean.reshape(B, nb, W).astype(MXU_DTYPE)
    return pl.pallas_call(
        functools.partial(_moba_kernel, nb=nb, TK=TK, RC=RC),
        grid=(B, nb),
        in_specs=[pl.BlockSpec((1, W, MOBA_BLOCK), lambda b, i: (b, 0, i)),
                  _resident((1, S, W), lambda b, i: (b, 0, 0)),
                  _resident((1, W, S), lambda b, i: (b, 0, 0)),
                  _resident((1, nb, W), lambda b, i: (b, 0, 0))],
        out_specs=pl.BlockSpec((1, W, MOBA_BLOCK), lambda b, i: (b, 0, i)),
        out_shape=jax.ShapeDtypeStruct((B, W, S), F32),
        scratch_shapes=[pltpu.VMEM((2, nb, MOBA_BLOCK), F32), pltpu.VMEM((MOBA_BLOCK, MOBA_BLOCK), F32),
                        pltpu.VMEM((2, TK, MOBA_BLOCK), F32), pltpu.VMEM((TK, MOBA_BLOCK), MXU_DTYPE)],
        compiler_params=_params(("arbitrary", "arbitrary")),
        name="moba",
    )(qT, k, vT, km)


def _order_key(x):
    b = pltpu.bitcast(x, I32)
    return b ^ ((b >> 31) & 0x7FFFFFFF)


def _dsa_kernel(qT_ref, k_ref, vT_ref, qiT_ref, ki_ref, w_ref, o_ref, hi_ref, lo_ref, bias_ref, s_ref, p_ref,
                *, TQ, TK, CH, TKA, RC, topk):
    i = pl.program_id(1)
    n_t = ((i + 1) * TQ + TK - 1) // TK
    krow = lax.broadcasted_iota(I32, (TK, TQ), 0)
    qcol = i * TQ + lax.broadcasted_iota(I32, (TK, TQ), 1)

    def index_tile(j, _):
        ks = pl.ds(pl.multiple_of(j * TK, TK), TK)
        ki = ki_ref[0, ks, :]
        acc = jnp.zeros((TK, TQ), F32)
        for h in range(IDX_HEADS):
            s = _dot(ki, qiT_ref[0, h * IDX_DIM:(h + 1) * IDX_DIM, :])
            acc = acc + w_ref[0, h:h + 1, :] * jnp.maximum(s, 0.0)
        key = jnp.where(j * TK + krow <= qcol, _order_key(acc), INT_MIN)
        hi_ref[ks, :] = (key >> 16).astype(I16)
        lo_ref[ks, :] = ((key & 0xFFFF) - 32768).astype(I16)
        return 0
    lax.fori_loop(0, n_t, index_tile, 0)

    n_ch = n_t * (TK // CH)
    one, zero = jnp.ones((PACK_ROWS, TQ), I16), jnp.zeros((PACK_ROWS, TQ), I16)
    rows16 = lax.broadcasted_iota(I32, (PACK_ROWS, TQ), 0)

    def rows_to_i16(v):
        return jnp.broadcast_to(v, (PACK_ROWS, TQ)).astype(I16)

    n_sub = CH // PACK_ROWS

    def chunk_blocks(refs, c):
        r0 = pl.multiple_of(c * CH, CH)
        vals = [ref[pl.ds(r0, CH), :] for ref in refs]
        return [([v[u * PACK_ROWS:(u + 1) * PACK_ROWS] for v in vals], r0 + u * PACK_ROWS) for u in range(n_sub)]

    def count(refs, flag):
        def chunk(c, part):
            for blocks, r in chunk_blocks(refs, c):
                part = part + flag(*blocks, r)
            return part
        part = lax.fori_loop(0, n_ch, chunk, zero)
        return jnp.sum(part.astype(I32), axis=0, keepdims=True)

    def search16(ref, k_needed):
        def step(t, ans_u):
            cand_u = ans_u | lax.shift_left(jnp.int32(1), 15 - t)
            c16 = rows_to_i16(cand_u - 32768)
            cnt = count([ref], lambda x, r: jnp.where(x >= c16, one, zero))
            return jnp.where(cnt >= k_needed, cand_u, ans_u)
        return lax.fori_loop(0, 16, step, jnp.zeros((1, TQ), I32)) - 32768

    P = search16(hi_ref, topk)
    P = jnp.maximum(P, -32767)
    P16 = rows_to_i16(P)
    n_hi_gt = count([hi_ref], lambda hi, r: jnp.where(hi > P16, one, zero))

    lowest = jnp.full((PACK_ROWS, TQ), -32768, I16)

    def restrict_lo(c, _):
        out = [jnp.where(hi == P16, lo, lowest) for (hi, lo), _ in chunk_blocks([hi_ref, lo_ref], c)]
        lo_ref[pl.ds(pl.multiple_of(c * CH, CH), CH), :] = jnp.concatenate(out, axis=0)
        return 0
    lax.fori_loop(0, n_ch, restrict_lo, 0)
    Q = search16(lo_ref, topk - n_hi_gt)
    Q16 = rows_to_i16(Q)

    def in_bucket_eq(hi, lo):
        return jnp.where(hi == P16, jnp.where(lo == Q16, one, zero), zero)
    n_gt = n_hi_gt + count([lo_ref], lambda lo, r: jnp.where(lo > Q16, one, zero))
    n_eq = count([hi_ref, lo_ref], lambda hi, lo, r: in_bucket_eq(hi, lo))
    need = topk - n_gt
    tied = n_eq > need
    any_tied = jnp.max(jnp.where(tied, 1, 0)) > 0

    def last_tie_row(_):
        def step(t, ans):
            cand = ans | lax.shift_left(jnp.int32(1), 14 - t)
            c16 = rows_to_i16(cand)
            cnt = count([hi_ref, lo_ref],
                        lambda hi, lo, r: jnp.where((r + rows16).astype(I16) < c16, in_bucket_eq(hi, lo), zero))
            return jnp.where(cnt < need, cand, ans)
        return lax.fori_loop(0, 15, step, jnp.zeros((1, TQ), I32))
    jmax = lax.cond(any_tied, last_tie_row, lambda _: jnp.full((1, TQ), 32767, I32), 0)
    J16 = rows_to_i16(jnp.where(tied, jmax, 32767))

    def to_bias(c, _):
        out = []
        for (hi, lo), r in chunk_blocks([hi_ref, lo_ref], c):
            tie_keep = jnp.where((r + rows16).astype(I16) <= J16, one, zero)
            in_p = jnp.where(lo > Q16, one, jnp.where(lo == Q16, tie_keep, zero))
            keep = jnp.where(hi > P16, one, jnp.where(hi == P16, in_p, zero))
            out.append(jnp.where(keep.astype(I32) > 0, 0.0, NEG))
        bias_ref[pl.ds(pl.multiple_of(c * CH, CH), CH), :] = jnp.concatenate(out, axis=0)
        return 0
    lax.fori_loop(0, n_ch, to_bias, 0)

    tiles_per_att = TKA // TK
    n_att = (n_t + tiles_per_att - 1) // tiles_per_att
    for extra in range(1, tiles_per_att):
        @pl.when(n_t % tiles_per_att == extra)
        def _():
            for t in range(tiles_per_att - extra):
                bias_ref[pl.ds(pl.multiple_of((n_t + t) * TK, TK), TK), :] = jnp.full((TK, TQ), NEG, F32)

    for p in range(DSA_HEADS // 2):
        ps = slice(p * PAIR_W, (p + 1) * PAIR_W)
        hrows = [slice((2 * p + r) * HEAD_DIM, (2 * p + r + 1) * HEAD_DIM) for r in range(2)]
        qpads = _pair_queries(qT_ref, p, TQ)
        outs = _pair_attention(k_ref, vT_ref, ps, hrows, qpads, n_att,
                               lambda r, ks, c: bias_ref[pl.ds(ks + c * RC, RC), :], s_ref, p_ref,
                               TK=TKA, RC=RC, TQ=TQ)
        for r in range(2):
            o_ref[0, hrows[r], :] = outs[r]


def _dsa(qT, k, vT, qiT, ki, wT):
    B, W, S = qT.shape
    assert S < 2 ** 15, "key indices are compared as int16"
    topk = min(DSA_TOPK, S // 4)
    TQ, TK, CH = 256, min(512, S), 256
    TKA, RC = min(1024, S), 128
    assert S % TKA == 0 and TKA % TK == 0
    return pl.pallas_call(
        functools.partial(_dsa_kernel, TQ=TQ, TK=TK, CH=CH, TKA=TKA, RC=RC, topk=topk),
        grid=(B, S // TQ),
        in_specs=[pl.BlockSpec((1, W, TQ), lambda b, i: (b, 0, i)),
                  _resident((1, S, W), lambda b, i: (b, 0, 0)),
                  _resident((1, W, S), lambda b, i: (b, 0, 0)),
                  pl.BlockSpec((1, IDX_HEADS * IDX_DIM, TQ), lambda b, i: (b, 0, i)),
                  _resident((1, S, IDX_DIM), lambda b, i: (b, 0, 0)),
                  pl.BlockSpec((1, IDX_HEADS, TQ), lambda b, i: (b, 0, i))],
        out_specs=pl.BlockSpec((1, W, TQ), lambda b, i: (b, 0, i)),
        out_shape=jax.ShapeDtypeStruct((B, W, S), F32),
        scratch_shapes=[pltpu.VMEM((S, TQ), I16), pltpu.VMEM((S, TQ), I16), pltpu.VMEM((S, TQ), F32),
                        pltpu.VMEM((2, TKA, TQ), F32), pltpu.VMEM((TKA, TQ), MXU_DTYPE)],
        compiler_params=_params(("arbitrary", "arbitrary")),
        name="dsa",
    )(qT, k, vT, qiT, ki, wT)


def _mix_kernel(x_ref, ym_ref, yd_ref, u_ref, up_ref, cw_ref, cb_ref, lg_ref, lb_ref, mg_ref, dg_ref,
                wom_ref, woc_ref, wod_ref, g1_ref, n2_ref, sc_ref, sh_ref, rgw_ref, rgb_ref, rew_ref, reb_ref,
                x1_ref, h2_ref, eid_ref, gate_ref, win_ref, *, tm, halo):
    i = pl.program_id(1)
    win_ref[0:halo, :] = jnp.where(i > 0, up_ref[0], 0.0)
    win_ref[halo:halo + tm, :] = u_ref[0]
    rc = 128
    off = halo - (CONV_WIDTH - 1)
    cw = cw_ref[...]
    for r0 in range(0, tm, rc):
        acc = jnp.zeros((rc, CONV_CH), F32) + cb_ref[...]
        for k in range(CONV_WIDTH):
            acc = acc + cw[k:k + 1, :] * win_ref[r0 + off + k:r0 + off + k + rc, :]
        mu = jnp.mean(acc, axis=-1, keepdims=True)
        xc = acc - mu
        yn = xc * lax.rsqrt(jnp.mean(xc * xc, axis=-1, keepdims=True) + EPS) * lg_ref[...] + lb_ref[...]
        win_ref[halo + tm + r0:halo + tm + r0 + rc, :] = yn * jax.nn.sigmoid(yn)
    yc = win_ref[halo + tm:halo + 2 * tm, :]

    ymn = (_rms(ym_ref[0].T) * mg_ref[...]).astype(MXU_DTYPE)
    ydn = (_rms(yd_ref[0].T) * dg_ref[...]).astype(MXU_DTYPE)
    proj = _dot(ymn, wom_ref[...]) + _dot(yc.astype(MXU_DTYPE), woc_ref[...]) + _dot(ydn, wod_ref[...])
    x1 = x_ref[0] + g1_ref[0] * proj
    x1_ref[0] = x1
    h2 = _rms(x1) * n2_ref[...] * (1.0 + sc_ref[0]) + sh_ref[0]
    h2_ref[0] = h2

    lane = lax.broadcasted_iota(I32, (tm, LANES), 1)
    gl = _dot(h2, rgw_ref[...]) + rgb_ref[...]
    gl = jnp.where(lane < N_GROUPS, gl, -jnp.inf)
    gmax = jnp.max(gl, axis=-1, keepdims=True)
    g_idx = jnp.min(jnp.where(gl == gmax, lane, LANES), axis=-1, keepdims=True)
    g_w = 1.0 / jnp.sum(jnp.exp(gl - gmax), axis=-1, keepdims=True)
    el = _dot(h2, rew_ref[...]) + reb_ref[...]
    in_group = jnp.logical_and(lane >= g_idx * EXPERTS_PER_GROUP, lane < (g_idx + 1) * EXPERTS_PER_GROUP)
    el = jnp.where(in_group, el, -jnp.inf)
    t1 = jnp.max(el, axis=-1, keepdims=True)
    e1 = jnp.min(jnp.where(el == t1, lane, LANES), axis=-1, keepdims=True)
    el2 = jnp.where(lane == e1, -jnp.inf, el)
    t2 = jnp.max(el2, axis=-1, keepdims=True)
    e2 = jnp.min(jnp.where(el2 == t2, lane, LANES), axis=-1, keepdims=True)
    r = jnp.exp(t2 - t1)
    p1 = 1.0 / (1.0 + r)
    lane8 = lax.broadcasted_iota(I32, (tm, 8), 1)
    eid_ref[0] = jnp.where(lane8 == 0, e1, jnp.where(lane8 == 1, e2, 0))
    gate_ref[0] = jnp.where(lane8 == 0, g_w * p1, jnp.where(lane8 == 1, g_w * p1 * r, 0.0))


def _mix(x, ymT, ydT, u, p, g1, sc2, sh2):
    B, S, D = x.shape
    tm, halo = 512, 32
    row = lambda width: pl.BlockSpec((1, tm, width), lambda b, i: (b, i, 0))
    colT = pl.BlockSpec((1, ATT_W, tm), lambda b, i: (b, 0, i))
    vec = pl.BlockSpec((1, 1, D), lambda b, i: (b, 0, 0))
    full = lambda a: pl.BlockSpec(a.shape, lambda b, i: (0,) * a.ndim)
    wo = p["w_out"].astype(MXU_DTYPE)
    wom, woc, wod = wo[:ATT_W], wo[ATT_W:ATT_W + CONV_CH], wo[ATT_W + CONV_CH:]
    pad_cols = lambda w: jnp.zeros((w.shape[0], LANES), F32).at[:, :w.shape[1]].set(w)
    pad_row = lambda v: jnp.zeros((1, LANES), F32).at[0, :v.shape[0]].set(v)
    small = [p["conv_w"], p["conv_b"].reshape(1, -1), p["conv_ln_g"].reshape(1, -1),
             p["conv_ln_b"].reshape(1, -1), p["moba_norm_g"].reshape(1, -1), p["dsa_norm_g"].reshape(1, -1),
             wom, woc, wod]
    tail = [p["norm2_g"].reshape(1, D)]
    rt = [pad_cols(p["router_group_w"]), pad_row(p["router_group_b"]),
          pad_cols(p["router_expert_w"]), pad_row(p["router_expert_b"])]
    sds = jax.ShapeDtypeStruct
    return pl.pallas_call(
        functools.partial(_mix_kernel, tm=tm, halo=halo),
        grid=(B, S // tm),
        in_specs=[row(D), colT, colT, row(CONV_CH),
                  pl.BlockSpec((1, halo, CONV_CH), lambda b, i: (b, jnp.maximum(i * (tm // halo) - 1, 0), 0))]
                 + [full(a) for a in small] + [vec] + [full(a) for a in tail] + [vec, vec] + [full(a) for a in rt],
        out_specs=[row(D), row(D), row(8), row(8)],
        out_shape=[sds((B, S, D), F32), sds((B, S, D), F32), sds((B, S, 8), I32), sds((B, S, 8), F32)],
        scratch_shapes=[pltpu.VMEM((halo + 2 * tm, CONV_CH), F32)],
        compiler_params=_params(("arbitrary", "arbitrary")),
        name="mix_outproj_route",
    )(x, ymT, ydT, u, u, *small, g1, *tail, sc2, sh2, *rt)


def _ffn_kernel(blk_e_ref, nvalid_ref, rowinfo_ref, h_hbm, gate_ref, w1_ref, w3_ref, w2_ref, y_hbm,
                xbuf, ybuf, gsem, ssem, *, T):
    i = pl.program_id(0)
    nv = nvalid_ref[i]
    base = i * MOE_BLOCK

    @pl.when(nv > 0)
    def _():
        def gather(r, _):
            tok = rowinfo_ref[base + r] >> 1
            pltpu.make_async_copy(h_hbm.at[pl.ds(tok, 1)], xbuf.at[pl.ds(r, 1)], gsem).start()
            return 0
        lax.fori_loop(0, MOE_BLOCK, gather, 0)
        pltpu.make_async_copy(h_hbm.at[pl.ds(0, MOE_BLOCK)], xbuf, gsem).wait()

        xb = xbuf[...].astype(MXU_DTYPE)
        a = _dot(xb, w1_ref[0])
        hid = (a * jax.nn.sigmoid(a)) * _dot(xb, w3_ref[0])
        ybuf[...] = _dot(hid.astype(MXU_DTYPE), w2_ref[0]) * gate_ref[0]

        def scatter(r, _):
            info = rowinfo_ref[base + r]
            dst = (info & 1) * T + (info >> 1)
            pltpu.make_async_copy(ybuf.at[pl.ds(r, 1)], y_hbm.at[pl.ds(dst, 1)], ssem).start()
            return 0
        lax.fori_loop(0, nv, scatter, 0)

        def scatter_wait(r, _):
            pltpu.make_async_copy(ybuf.at[pl.ds(0, 1)], y_hbm.at[pl.ds(0, 1)], ssem).wait()
            return 0
        lax.fori_loop(0, nv, scatter_wait, 0)


def _moe(h2, eid, gates, w1, w3, w2):
    T, D = h2.shape
    n_slots = 2 * T
    P = n_slots + N_EXPERTS * MOE_BLOCK
    nblk = P // MOE_BLOCK
    flat_e = eid.reshape(-1)
    onehot = (flat_e[:, None] == jnp.arange(N_EXPERTS, dtype=I32)[None, :]).astype(I32)
    csum = jnp.cumsum(onehot, axis=0)
    rank = jnp.sum(csum * onehot, axis=1) - 1
    counts = csum[-1]
    padded = (counts + MOE_BLOCK - 1) // MOE_BLOCK * MOE_BLOCK
    pad_end = jnp.cumsum(padded)
    pad_start = pad_end - padded
    dest = pad_start[flat_e] + rank
    rowinfo = jnp.zeros((P,), I32).at[dest].set(jnp.arange(n_slots, dtype=I32))
    rowgate = jnp.zeros((P,), F32).at[dest].set(gates.reshape(-1))
    blk_start = jnp.arange(nblk, dtype=I32) * MOE_BLOCK
    blk_e = jnp.minimum(jnp.searchsorted(pad_end, blk_start, side="right"), N_EXPERTS - 1).astype(I32)
    seg_end = (pad_start + counts)[blk_e]
    nvalid = jnp.clip(seg_end - blk_start, 0, MOE_BLOCK).astype(I32)
    nvalid = jnp.where(blk_start < pad_end[-1], nvalid, 0)

    grid_spec = pltpu.PrefetchScalarGridSpec(
        num_scalar_prefetch=3,
        grid=(nblk,),
        in_specs=[pl.BlockSpec(memory_space=pl.ANY),
                  pl.BlockSpec((1, MOE_BLOCK, 1), lambda i, be, nv, ri: (i, 0, 0)),
                  pl.BlockSpec((1, D, D_EXPERT), lambda i, be, nv, ri: (be[i], 0, 0)),
                  pl.BlockSpec((1, D, D_EXPERT), lambda i, be, nv, ri: (be[i], 0, 0)),
                  pl.BlockSpec((1, D_EXPERT, D), lambda i, be, nv, ri: (be[i], 0, 0))],
        out_specs=pl.BlockSpec(memory_space=pl.ANY),
        scratch_shapes=[pltpu.VMEM((MOE_BLOCK, D), F32), pltpu.VMEM((MOE_BLOCK, D), F32),
                        pltpu.SemaphoreType.DMA, pltpu.SemaphoreType.DMA])
    y = pl.pallas_call(
        functools.partial(_ffn_kernel, T=T),
        grid_spec=grid_spec,
        out_shape=jax.ShapeDtypeStruct((2 * T, D), F32),
        compiler_params=_params(("arbitrary",)),
        name="moe_ffn",
    )(blk_e, nvalid, rowinfo, h2, rowgate.reshape(nblk, MOE_BLOCK, 1), w1, w3, w2)
    return y.reshape(2, T, D)


def _final_kernel(x_ref, y_ref, g2_ref, fg_ref, o_ref):
    x = x_ref[0] + g2_ref[0] * (y_ref[0, 0] + y_ref[1, 0])
    o_ref[0] = _rms(x) * fg_ref[...]


def _final(x1, y, g2, final_g):
    B, S, D = x1.shape
    tm = 512
    return pl.pallas_call(
        _final_kernel,
        grid=(B, S // tm),
        in_specs=[pl.BlockSpec((1, tm, D), lambda b, i: (b, i, 0)),
                  pl.BlockSpec((2, 1, tm, D), lambda b, i: (0, b, i, 0)),
                  pl.BlockSpec((1, 1, D), lambda b, i: (b, 0, 0)),
                  pl.BlockSpec((1, D), lambda b, i: (0, 0))],
        out_specs=pl.BlockSpec((1, tm, D), lambda b, i: (b, i, 0)),
        out_shape=jax.ShapeDtypeStruct((B, S, D), F32),
        compiler_params=_params(("arbitrary", "arbitrary")),
        name="final_norm",
    )(x1, y, g2, final_g.reshape(1, D))


def kernel(x, c, ada_w, ada_b, norm1_g, w_in, conv_w, conv_b, conv_ln_g, conv_ln_b, moba_norm_g, dsa_norm_g, w_out, norm2_g, router_group_w, router_group_b, router_expert_w, router_expert_b, expert_w1, expert_w3, expert_w2, final_g):
    B, S, D = x.shape
    L = ada_w.shape[0]
    mod = _modulation(c, ada_w, ada_b)
    comb = None
    for l in range(L):
        sh1, sc1, g1, sh2, sc2, g2 = [m[:, None, :] for m in jnp.split(mod[l], 6, axis=-1)]
        x, (qmT, km, vmT, kmean, u, qdT, kd, vdT, qiT, ki, wT) = _inproj(
            x, comb, norm1_g[l], sc1, sh1, _split_w_in(w_in[l]))
        y_moba = _moba(qmT, km, vmT, kmean)
        y_dsa = _dsa(qdT, kd, vdT, qiT, ki, wT)
        p = dict(conv_w=conv_w[l], conv_b=conv_b[l], conv_ln_g=conv_ln_g[l], conv_ln_b=conv_ln_b[l],
                 moba_norm_g=moba_norm_g[l], dsa_norm_g=dsa_norm_g[l], w_out=w_out[l], norm2_g=norm2_g[l],
                 router_group_w=router_group_w[l], router_group_b=router_group_b[l],
                 router_expert_w=router_expert_w[l], router_expert_b=router_expert_b[l])
        x1, h2, eid, gates = _mix(x, y_moba, y_dsa, u, p, g1, sc2, sh2)
        y = _moe(h2.reshape(B * S, D), eid.reshape(B * S, 8)[:, :2], gates.reshape(B * S, 8)[:, :2],
                 expert_w1[l].astype(MXU_DTYPE), expert_w3[l].astype(MXU_DTYPE), expert_w2[l].astype(MXU_DTYPE))
        x = x1
        comb = (y.reshape(2, B, S, D), g2)
    return _final(x, comb[0], comb[1], final_g)
```

```python
import functools

import jax
import jax.numpy as jnp
from jax import lax
from jax.experimental import pallas as pl
from jax.experimental.pallas import tpu as pltpu

F32 = jnp.float32
I32 = jnp.int32
I16 = jnp.int16
MXU_DTYPE = jnp.bfloat16

HEAD_DIM = 64
MOBA_HEADS = 6
MOBA_BLOCK = 256
MOBA_TOPK = 3
CONV_CH = 256
CONV_WIDTH = 31
DSA_HEADS = 6
DSA_TOPK = 256
IDX_HEADS = 8
IDX_DIM = 64
ATT_W = 384
N_GROUPS = 4
EXPERTS_PER_GROUP = 8
N_EXPERTS = 32
D_EXPERT = 512
MOE_BLOCK = 256
EPS = 1e-6

LANES = 128
PACK_ROWS = 16
PAIR_W = 2 * HEAD_DIM
NEG = -1e30
INT_MIN = -2 ** 31
VMEM_LIMIT = 56 * 1024 * 1024
Q_SCALE = HEAD_DIM ** -0.5
IDX_SCALE = IDX_HEADS ** -0.5 * IDX_DIM ** -0.5


def _params(sem):
    return pltpu.CompilerParams(dimension_semantics=sem, vmem_limit_bytes=VMEM_LIMIT)


def _rms(xf):
    return xf * lax.rsqrt(jnp.mean(xf * xf, axis=-1, keepdims=True) + EPS)


def _dot(a, b):
    return jnp.dot(a, b, preferred_element_type=F32)


def _dot_nt(a, b):
    return lax.dot_general(a, b, (((1,), (1,)), ((), ())), preferred_element_type=F32)


def _resident(block_shape, index_map):
    return pl.BlockSpec(block_shape, index_map, pipeline_mode=pl.Buffered(1))


def _mod_kernel(c_ref, w_ref, b_ref, o_ref):
    c = c_ref[...]
    o_ref[0] = _dot(c * jax.nn.sigmoid(c), w_ref[0]) + b_ref[0]


def _modulation(c, ada_w, ada_b):
    L, D, N = ada_w.shape
    B = c.shape[0]
    rows = 8
    cp = jnp.zeros((rows, D), F32).at[:B].set(c)
    tn = 1536
    out = pl.pallas_call(
        _mod_kernel,
        grid=(L, N // tn),
        in_specs=[pl.BlockSpec((rows, D), lambda l, j: (0, 0)),
                  pl.BlockSpec((1, D, tn), lambda l, j: (l, 0, j)),
                  pl.BlockSpec((1, 1, tn), lambda l, j: (l, 0, j))],
        out_specs=pl.BlockSpec((1, rows, tn), lambda l, j: (l, 0, j)),
        out_shape=jax.ShapeDtypeStruct((L, rows, N), F32),
        compiler_params=_params(("arbitrary", "arbitrary")),
        name="adaln_mod",
    )(cp, ada_w, ada_b.reshape(L, 1, N))
    return out[:, :B]


def _inproj_kernel(x_ref, g_ref, sc_ref, sh_ref, wqmT, wkm, wvmT, wglu, wqdT, wkd, wvdT, wqiT, wki, wwiT,
                   qmT_ref, km_ref, vmT_ref, kmean_ref, u_ref, qdT_ref, kd_ref, vdT_ref, qiT_ref, ki_ref, wiT_ref,
                   *, tm):
    h = _rms(x_ref[0]) * g_ref[...] * (1.0 + sc_ref[0]) + sh_ref[0]
    hb = h.astype(MXU_DTYPE)

    qmT_ref[0] = _dot_nt(wqmT[...], hb).astype(qmT_ref.dtype)
    k_rows = _dot(hb, wkm[...])
    km_ref[0] = k_rows.astype(km_ref.dtype)
    for c in range(tm // MOBA_BLOCK):
        kmean_ref[0, c] = jnp.mean(k_rows[c * MOBA_BLOCK:(c + 1) * MOBA_BLOCK], axis=0, keepdims=True)
    vmT_ref[0] = _dot_nt(wvmT[...], hb).astype(vmT_ref.dtype)
    glu = _dot(hb, wglu[...])
    u_ref[0] = glu[:, :CONV_CH] * jax.nn.sigmoid(glu[:, CONV_CH:])
    qdT_ref[0] = _dot_nt(wqdT[...], hb).astype(qdT_ref.dtype)
    kd_ref[0] = _dot(hb, wkd[...]).astype(kd_ref.dtype)
    vdT_ref[0] = _dot_nt(wvdT[...], hb).astype(vdT_ref.dtype)
    qiT_ref[0] = _dot_nt(wqiT[...], hb).astype(qiT_ref.dtype)
    ki_ref[0] = _dot(hb, wki[...]).astype(ki_ref.dtype)
    wiT_ref[0] = _dot_nt(wwiT[...], hb) * IDX_SCALE


_W_NAMES = ["wqmT", "wkm", "wvmT", "wglu", "wqdT", "wkd", "wvdT", "wqiT", "wki", "wwiT"]


def _split_w_in(w_in):
    offs = [0, 384, 768, 1152, 1408, 1664, 2048, 2432, 2816, 3328, 3392, 3400]
    piece = lambda k: w_in[:, offs[k]:offs[k + 1]]
    c = lambda w: w.astype(MXU_DTYPE)
    return dict(
        wqmT=c(piece(0).T * Q_SCALE), wkm=c(piece(1)), wvmT=c(piece(2).T),
        wglu=c(w_in[:, offs[3]:offs[5]]),
        wqdT=c(piece(5).T * Q_SCALE), wkd=c(piece(6)), wvdT=c(piece(7).T),
        wqiT=c(piece(8).T), wki=c(piece(9)), wwiT=c(piece(10).T))


def _inproj(x, n1g, sc1, sh1, w):
    B, S, D = x.shape
    tm = 512
    nb_t = tm // MOBA_BLOCK
    row = lambda width: pl.BlockSpec((1, tm, width), lambda b, i: (b, i, 0))
    col = lambda height: pl.BlockSpec((1, height, tm), lambda b, i: (b, 0, i))
    vec = pl.BlockSpec((1, 1, D), lambda b, i: (b, 0, 0))
    full = lambda a: pl.BlockSpec(a.shape, lambda b, i: (0,) * a.ndim)

    ins = [x, n1g.reshape(1, D), sc1, sh1] + [w[k] for k in _W_NAMES]
    in_specs = [row(D), pl.BlockSpec((1, D), lambda b, i: (0, 0)), vec, vec] + [full(w[k]) for k in _W_NAMES]

    sds = jax.ShapeDtypeStruct
    QI_W = IDX_HEADS * IDX_DIM
    out_shape = [
        sds((B, ATT_W, S), MXU_DTYPE), sds((B, S, ATT_W), MXU_DTYPE), sds((B, ATT_W, S), MXU_DTYPE),
        sds((B, S // MOBA_BLOCK, 1, ATT_W), F32),
        sds((B, S, CONV_CH), F32),
        sds((B, ATT_W, S), MXU_DTYPE), sds((B, S, ATT_W), MXU_DTYPE), sds((B, ATT_W, S), MXU_DTYPE),
        sds((B, QI_W, S), MXU_DTYPE), sds((B, S, IDX_DIM), MXU_DTYPE), sds((B, IDX_HEADS, S), F32)]
    out_specs = [
        col(ATT_W), row(ATT_W), col(ATT_W),
        pl.BlockSpec((1, nb_t, 1, ATT_W), lambda b, i: (b, i, 0, 0)),
        row(CONV_CH),
        col(ATT_W), row(ATT_W), col(ATT_W),
        col(QI_W), row(IDX_DIM), col(IDX_HEADS)]

    return pl.pallas_call(
        functools.partial(_inproj_kernel, tm=tm),
        grid=(B, S // tm),
        in_specs=in_specs, out_specs=out_specs, out_shape=out_shape,
        compiler_params=_params(("arbitrary", "arbitrary")),
        name="inproj",
    )(*ins)


def _pair_queries(qT_ref, p, TQ):
    first = lax.broadcasted_iota(I32, (PAIR_W, TQ), 0) < HEAD_DIM
    qpair = qT_ref[0, p * PAIR_W:(p + 1) * PAIR_W, :]
    zero = jnp.zeros_like(qpair)
    return jnp.where(first, qpair, zero), jnp.where(first, zero, qpair)


def _fold8(x, op):
    return op(x.reshape(x.shape[0] // 8, 8, x.shape[1]), axis=0)


def _pair_attention(k_ref, vT_ref, ps, hrows, qpads, n_tiles, bias_fn, s_ref, p_ref, *, TK, RC, TQ):
    n_chunks = TK // RC

    def scores(ks, c, r):
        rows = pl.ds(ks + c * RC, RC)
        return _dot(k_ref[0, rows, ps], qpads[r]) + bias_fn(r, ks, c)

    def section(ks_next, slot_next, ks_cur, slot_cur, m_new, m_old, l, acc):
        runmax = jnp.full((8, TQ), NEG, F32)
        lsum = jnp.zeros((8, TQ), F32)
        for c in range(n_chunks):
            cs = slice(c * RC, (c + 1) * RC)
            s = scores(ks_next, c, slot_next)
            s_ref[slot_next, cs, :] = s
            runmax = jnp.maximum(runmax, _fold8(s, jnp.max))
            pc = jnp.exp(s_ref[slot_cur, cs, :] - m_new)
            lsum = lsum + _fold8(pc, jnp.sum)
            p_ref[cs, :] = pc.astype(p_ref.dtype)
        alpha = jnp.exp(m_old - m_new)
        l = alpha * l + jnp.sum(lsum, axis=0, keepdims=True)
        acc = alpha * acc + _dot(vT_ref[0, hrows[slot_cur], pl.ds(ks_cur, TK)], p_ref[...])
        return jnp.max(runmax, axis=0, keepdims=True), l, acc

    runmax = jnp.full((8, TQ), NEG, F32)
    for c in range(n_chunks):
        s = scores(0, c, 0)
        s_ref[0, c * RC:(c + 1) * RC, :] = s
        runmax = jnp.maximum(runmax, _fold8(s, jnp.max))
    tmax0 = jnp.max(runmax, axis=0, keepdims=True)

    def tile(j, carry):
        (m0, l0, a0), (m1, l1, a1), tmax = carry
        ks = pl.multiple_of(j * TK, TK)
        ks_next = pl.multiple_of(jnp.minimum(j + 1, n_tiles - 1) * TK, TK)
        m0n = jnp.maximum(m0, tmax)
        tmax1, l0, a0 = section(ks, 1, ks, 0, m0n, m0, l0, a0)
        m1n = jnp.maximum(m1, tmax1)
        tmax0n, l1, a1 = section(ks_next, 0, ks, 1, m1n, m1, l1, a1)
        return (m0n, l0, a0), (m1n, l1, a1), tmax0n

    init = (jnp.full((1, TQ), NEG, F32), jnp.zeros((1, TQ), F32), jnp.zeros((HEAD_DIM, TQ), F32))
    (_, l0, a0), (_, l1, a1), _ = lax.fori_loop(0, n_tiles, tile, (init, init, tmax0))
    return a0 / l0, a1 / l1


def _moba_kernel(qT_ref, k_ref, vT_ref, km_ref, o_ref, sb_ref, cb_ref, s_ref, p_ref, *, nb, TK, RC):
    TQ = MOBA_BLOCK
    i = pl.program_id(1)
    blk = lax.broadcasted_iota(I32, (nb, TQ), 0)
    past = blk < i
    cb_ref[...] = jnp.where(lax.broadcasted_iota(I32, (TQ, TQ), 0) <= lax.broadcasted_iota(I32, (TQ, TQ), 1),
                            0.0, NEG)
    blocks_per_tile = TK // TQ
    n_tiles = (i + blocks_per_tile) // blocks_per_tile

    for p in range(MOBA_HEADS // 2):
        ps = slice(p * PAIR_W, (p + 1) * PAIR_W)
        hrows = [slice((2 * p + r) * HEAD_DIM, (2 * p + r + 1) * HEAD_DIM) for r in range(2)]
        qpads = _pair_queries(qT_ref, p, TQ)
        for r in range(2):
            g = jnp.where(past, _dot(km_ref[0, :, ps], qpads[r]), -jnp.inf)
            sel = jnp.zeros((nb, TQ), I32)
            for _ in range(MOBA_TOPK):
                mx = jnp.max(g, axis=0, keepdims=True)
                first = jnp.min(jnp.where(g == mx, blk, nb), axis=0, keepdims=True)
                hit = blk == first
                sel = jnp.where(hit, 1, sel)
                g = jnp.where(hit, -jnp.inf, g)
            sb_ref[r] = jnp.where(past, jnp.where(sel > 0, 0.0, NEG), NEG)

        def bias_fn(r, ks, c):
            b = ks // TQ + (c * RC) // TQ
            within = (c * RC) % TQ
            row = jnp.broadcast_to(sb_ref[r, pl.ds(b, 1), :], (RC, TQ))
            return jnp.where(b == i, cb_ref[within:within + RC, :], row)

        outs = _pair_attention(k_ref, vT_ref, ps, hrows, qpads, n_tiles, bias_fn, s_ref, p_ref,
                               TK=TK, RC=RC, TQ=TQ)
        for r in range(2):
            o_ref[0, hrows[r], :] = outs[r]


def _moba(qT, k, vT, kmean):
    B, W, S = qT.shape
    nb = S // MOBA_BLOCK
    TK, RC = min(1024, S), 128
    assert S % TK == 0 and TK % MOBA_BLOCK == 0 and MOBA_BLOCK % RC == 0
    km = kmean.reshape(B, nb, W).astype(MXU_DTYPE)
    return pl.pallas_call(
        functools.partial(_moba_kernel, nb=nb, TK=TK, RC=RC),
        grid=(B, nb),
        in_specs=[pl.BlockSpec((1, W, MOBA_BLOCK), lambda b, i: (b, 0, i)),
                  _resident((1, S, W), lambda b, i: (b, 0, 0)),
                  _resident((1, W, S), lambda b, i: (b, 0, 0)),
                  _resident((1, nb, W), lambda b, i: (b, 0, 0))],
        out_specs=pl.BlockSpec((1, W, MOBA_BLOCK), lambda b, i: (b, 0, i)),
        out_shape=jax.ShapeDtypeStruct((B, W, S), F32),
        scratch_shapes=[pltpu.VMEM((2, nb, MOBA_BLOCK), F32), pltpu.VMEM((MOBA_BLOCK, MOBA_BLOCK), F32),
                        pltpu.VMEM((2, TK, MOBA_BLOCK), F32), pltpu.VMEM((TK, MOBA_BLOCK), MXU_DTYPE)],
        compiler_params=_params(("arbitrary", "arbitrary")),
        name="moba",
    )(qT, k, vT, km)


def _order_key(x):
    b = pltpu.bitcast(x, I32)
    return b ^ ((b >> 31) & 0x7FFFFFFF)


def _dsa_kernel(qT_ref, k_ref, vT_ref, qiT_ref, ki_ref, w_ref, o_ref, hi_ref, lo_ref, bias_ref, s_ref, p_ref,
                *, TQ, TK, CH, TKA, RC, topk):
    i = pl.program_id(1)
    n_t = ((i + 1) * TQ + TK - 1) // TK
    krow = lax.broadcasted_iota(I32, (TK, TQ), 0)
    qcol = i * TQ + lax.broadcasted_iota(I32, (TK, TQ), 1)

    def index_tile(j, _):
        ks = pl.ds(pl.multiple_of(j * TK, TK), TK)
        ki = ki_ref[0, ks, :]
        acc = jnp.zeros((TK, TQ), F32)
        for h in range(IDX_HEADS):
            s = _dot(ki, qiT_ref[0, h * IDX_DIM:(h + 1) * IDX_DIM, :])
            acc = acc + w_ref[0, h:h + 1, :] * jnp.maximum(s, 0.0)
        key = jnp.where(j * TK + krow <= qcol, _order_key(acc), INT_MIN)
        hi_ref[ks, :] = (key >> 16).astype(I16)
        lo_ref[ks, :] = ((key & 0xFFFF) - 32768).astype(I16)
        return 0
    lax.fori_loop(0, n_t, index_tile, 0)

    n_ch = n_t * (TK // CH)
    one, zero = jnp.ones((PACK_ROWS, TQ), I16), jnp.zeros((PACK_ROWS, TQ), I16)
    rows16 = lax.broadcasted_iota(I32, (PACK_ROWS, TQ), 0)

    def rows_to_i16(v):
        return jnp.broadcast_to(v, (PACK_ROWS, TQ)).astype(I16)

    n_sub = CH // PACK_ROWS

    def chunk_blocks(refs, c):
        r0 = pl.multiple_of(c * CH, CH)
        vals = [ref[pl.ds(r0, CH), :] for ref in refs]
        return [([v[u * PACK_ROWS:(u + 1) * PACK_ROWS] for v in vals], r0 + u * PACK_ROWS) for u in range(n_sub)]

    def count(refs, flag):
        def chunk(c, parts):
            parts = list(parts)
            for u, (blocks, r) in enumerate(chunk_blocks(refs, c)):
                parts[u % len(parts)] = parts[u % len(parts)] + flag(*blocks, r)
            return tuple(parts)
        parts = lax.fori_loop(0, n_ch, chunk, (zero,) * 4)
        part = (parts[0] + parts[1]) + (parts[2] + parts[3])
        return jnp.sum(part.astype(I32), axis=0, keepdims=True)

    def search16(ref, k_needed):
        def step(t, ans_u):
            cand_u = ans_u | lax.shift_left(jnp.int32(1), 15 - t)
            c16 = rows_to_i16(cand_u - 32768)
            cnt = count([ref], lambda x, r: jnp.where(x >= c16, one, zero))
            return jnp.where(cnt >= k_needed, cand_u, ans_u)
        return lax.fori_loop(0, 16, step, jnp.zeros((1, TQ), I32)) - 32768

    P = search16(hi_ref, topk)
    P = jnp.maximum(P, -32767)
    P16 = rows_to_i16(P)
    n_hi_gt = count([hi_ref], lambda hi, r: jnp.where(hi > P16, one, zero))

    lowest = jnp.full((PACK_ROWS, TQ), -32768, I16)

    def restrict_lo(c, _):
        out = [jnp.where(hi == P16, lo, lowest) for (hi, lo), _ in chunk_blocks([hi_ref, lo_ref], c)]
        lo_ref[pl.ds(pl.multiple_of(c * CH, CH), CH), :] = jnp.concatenate(out, axis=0)
        return 0
    lax.fori_loop(0, n_ch, restrict_lo, 0)
    Q = search16(lo_ref, topk - n_hi_gt)
    Q16 = rows_to_i16(Q)

    def in_bucket_eq(hi, lo):
        return jnp.where(hi == P16, jnp.where(lo == Q16, one, zero), zero)
    n_gt = n_hi_gt + count([lo_ref], lambda lo, r: jnp.where(lo > Q16, one, zero))
    n_eq = count([hi_ref, lo_ref], lambda hi, lo, r: in_bucket_eq(hi, lo))
    need = topk - n_gt
    tied = n_eq > need
    any_tied = jnp.max(jnp.where(tied, 1, 0)) > 0

    def last_tie_row(_):
        def step(t, ans):
            cand = ans | lax.shift_left(jnp.int32(1), 14 - t)
            c16 = rows_to_i16(cand)
            cnt = count([hi_ref, lo_ref],
                        lambda hi, lo, r: jnp.where((r + rows16).astype(I16) < c16, in_bucket_eq(hi, lo), zero))
            return jnp.where(cnt < need, cand, ans)
        return lax.fori_loop(0, 15, step, jnp.zeros((1, TQ), I32))
    jmax = lax.cond(any_tied, last_tie_row, lambda _: jnp.full((1, TQ), 32767, I32), 0)
    J16 = rows_to_i16(jnp.where(tied, jmax, 32767))

    def to_bias(c, _):
        out = []
        for (hi, lo), r in chunk_blocks([hi_ref, lo_ref], c):
            tie_keep = jnp.where((r + rows16).astype(I16) <= J16, one, zero)
            in_p = jnp.where(lo > Q16, one, jnp.where(lo == Q16, tie_keep, zero))
            keep = jnp.where(hi > P16, one, jnp.where(hi == P16, in_p, zero))
            out.append(jnp.where(keep.astype(I32) > 0, 0.0, NEG))
        bias_ref[pl.ds(pl.multiple_of(c * CH, CH), CH), :] = jnp.concatenate(out, axis=0)
        return 0
    lax.fori_loop(0, n_ch, to_bias, 0)

    tiles_per_att = TKA // TK
    n_att = (n_t + tiles_per_att - 1) // tiles_per_att
    for extra in range(1, tiles_per_att):
        @pl.when(n_t % tiles_per_att == extra)
        def _():
            for t in range(tiles_per_att - extra):
                bias_ref[pl.ds(pl.multiple_of((n_t + t) * TK, TK), TK), :] = jnp.full((TK, TQ), NEG, F32)

    for p in range(DSA_HEADS // 2):
        ps = slice(p * PAIR_W, (p + 1) * PAIR_W)
        hrows = [slice((2 * p + r) * HEAD_DIM, (2 * p + r + 1) * HEAD_DIM) for r in range(2)]
        qpads = _pair_queries(qT_ref, p, TQ)
        outs = _pair_attention(k_ref, vT_ref, ps, hrows, qpads, n_att,
                               lambda r, ks, c: bias_ref[pl.ds(ks + c * RC, RC), :], s_ref, p_ref,
                               TK=TKA, RC=RC, TQ=TQ)
        for r in range(2):
            o_ref[0, hrows[r], :] = outs[r]


def _dsa(qT, k, vT, qiT, ki, wT):
    B, W, S = qT.shape
    assert S < 2 ** 15, "key indices are compared as int16"
    topk = min(DSA_TOPK, S // 4)
    TQ, TK, CH = 256, min(512, S), 256
    TKA, RC = min(1024, S), 128
    assert S % TKA == 0 and TKA % TK == 0
    return pl.pallas_call(
        functools.partial(_dsa_kernel, TQ=TQ, TK=TK, CH=CH, TKA=TKA, RC=RC, topk=topk),
        grid=(B, S // TQ),
        in_specs=[pl.BlockSpec((1, W, TQ), lambda b, i: (b, 0, i)),
                  _resident((1, S, W), lambda b, i: (b, 0, 0)),
                  _resident((1, W, S), lambda b, i: (b, 0, 0)),
                  pl.BlockSpec((1, IDX_HEADS * IDX_DIM, TQ), lambda b, i: (b, 0, i)),
                  _resident((1, S, IDX_DIM), lambda b, i: (b, 0, 0)),
                  pl.BlockSpec((1, IDX_HEADS, TQ), lambda b, i: (b, 0, i))],
        out_specs=pl.BlockSpec((1, W, TQ), lambda b, i: (b, 0, i)),
        out_shape=jax.ShapeDtypeStruct((B, W, S), F32),
        scratch_shapes=[pltpu.VMEM((S, TQ), I16), pltpu.VMEM((S, TQ), I16), pltpu.VMEM((S, TQ), F32),
                        pltpu.VMEM((2, TKA, TQ), F32), pltpu.VMEM((TKA, TQ), MXU_DTYPE)],
        compiler_params=_params(("arbitrary", "arbitrary")),
        name="dsa",
    )(qT, k, vT, qiT, ki, wT)


def _mix_kernel(x_ref, ym_ref, yd_ref, u_ref, up_ref, cw_ref, cb_ref, lg_ref, lb_ref, mg_ref, dg_ref,
                wom_ref, woc_ref, wod_ref, g1_ref, n2_ref, sc_ref, sh_ref, rgw_ref, rgb_ref, rew_ref, reb_ref,
                tril_ref, x1_ref, h2_ref, slot_ref, gate_ref, counts_ref, win_ref, cnt_ref, *, tm, halo):
    i = pl.program_id(1)
    win_ref[0:halo, :] = jnp.where(i > 0, up_ref[0], 0.0)
    win_ref[halo:halo + tm, :] = u_ref[0]
    rc = 128
    off = halo - (CONV_WIDTH - 1)
    cw = cw_ref[...]
    for r0 in range(0, tm, rc):
        acc = jnp.zeros((rc, CONV_CH), F32) + cb_ref[...]
        for k in range(CONV_WIDTH):
            acc = acc + cw[k:k + 1, :] * win_ref[r0 + off + k:r0 + off + k + rc, :]
        mu = jnp.mean(acc, axis=-1, keepdims=True)
        xc = acc - mu
        yn = xc * lax.rsqrt(jnp.mean(xc * xc, axis=-1, keepdims=True) + EPS) * lg_ref[...] + lb_ref[...]
        win_ref[halo + tm + r0:halo + tm + r0 + rc, :] = yn * jax.nn.sigmoid(yn)
    yc = win_ref[halo + tm:halo + 2 * tm, :]

    ymn = (_rms(ym_ref[0].T) * mg_ref[...]).astype(MXU_DTYPE)
    ydn = (_rms(yd_ref[0].T) * dg_ref[...]).astype(MXU_DTYPE)
    proj = _dot(ymn, wom_ref[...]) + _dot(yc.astype(MXU_DTYPE), woc_ref[...]) + _dot(ydn, wod_ref[...])
    x1 = x_ref[0] + g1_ref[0] * proj
    x1_ref[0] = x1
    h2 = _rms(x1) * n2_ref[...] * (1.0 + sc_ref[0]) + sh_ref[0]
    h2_ref[0] = h2

    lane = lax.broadcasted_iota(I32, (tm, LANES), 1)
    gl = _dot(h2, rgw_ref[...]) + rgb_ref[...]
    gl = jnp.where(lane < N_GROUPS, gl, -jnp.inf)
    gmax = jnp.max(gl, axis=-1, keepdims=True)
    g_idx = jnp.min(jnp.where(gl == gmax, lane, LANES), axis=-1, keepdims=True)
    g_w = 1.0 / jnp.sum(jnp.exp(gl - gmax), axis=-1, keepdims=True)
    el = _dot(h2, rew_ref[...]) + reb_ref[...]
    in_group = jnp.logical_and(lane >= g_idx * EXPERTS_PER_GROUP, lane < (g_idx + 1) * EXPERTS_PER_GROUP)
    el = jnp.where(in_group, el, -jnp.inf)
    t1 = jnp.max(el, axis=-1, keepdims=True)
    e1 = jnp.min(jnp.where(el == t1, lane, LANES), axis=-1, keepdims=True)
    el2 = jnp.where(lane == e1, -jnp.inf, el)
    t2 = jnp.max(el2, axis=-1, keepdims=True)
    e2 = jnp.min(jnp.where(el2 == t2, lane, LANES), axis=-1, keepdims=True)
    r = jnp.exp(t2 - t1)
    p1 = 1.0 / (1.0 + r)
    lane8 = lax.broadcasted_iota(I32, (tm, 8), 1)
    gate_ref[0] = jnp.where(lane8 == 0, g_w * p1, jnp.where(lane8 == 1, g_w * p1 * r, 0.0))

    @pl.when(jnp.logical_and(pl.program_id(0) == 0, i == 0))
    def _():
        cnt_ref[...] = jnp.zeros_like(cnt_ref)
    cnt = cnt_ref[...]
    ranks = []
    for e_k in (e1, e2):
        onehot = jnp.where(lane == e_k, 1.0, 0.0)
        before = cnt + _dot(tril_ref[...], onehot.astype(MXU_DTYPE))
        ranks.append(jnp.sum(onehot * before, axis=-1, keepdims=True).astype(I32))
        cnt = cnt + jnp.sum(onehot, axis=0, keepdims=True)
    cnt_ref[...] = cnt
    counts_ref[...] = cnt
    slot_ref[0] = jnp.where(lane8 == 0, e1, jnp.where(lane8 == 1, e2, jnp.where(
        lane8 == 2, ranks[0], jnp.where(lane8 == 3, ranks[1], 0))))


def _mix(x, ymT, ydT, u, p, g1, sc2, sh2):
    B, S, D = x.shape
    tm, halo = 512, 32
    row = lambda width: pl.BlockSpec((1, tm, width), lambda b, i: (b, i, 0))
    colT = pl.BlockSpec((1, ATT_W, tm), lambda b, i: (b, 0, i))
    vec = pl.BlockSpec((1, 1, D), lambda b, i: (b, 0, 0))
    full = lambda a: pl.BlockSpec(a.shape, lambda b, i: (0,) * a.ndim)
    wo = p["w_out"].astype(MXU_DTYPE)
    wom, woc, wod = wo[:ATT_W], wo[ATT_W:ATT_W + CONV_CH], wo[ATT_W + CONV_CH:]
    pad_cols = lambda w: jnp.zeros((w.shape[0], LANES), F32).at[:, :w.shape[1]].set(w)
    pad_row = lambda v: jnp.zeros((1, LANES), F32).at[0, :v.shape[0]].set(v)
    small = [p["conv_w"], p["conv_b"].reshape(1, -1), p["conv_ln_g"].reshape(1, -1),
             p["conv_ln_b"].reshape(1, -1), p["moba_norm_g"].reshape(1, -1), p["dsa_norm_g"].reshape(1, -1),
             wom, woc, wod]
    tail = [p["norm2_g"].reshape(1, D)]
    tril = (jnp.arange(tm)[:, None] > jnp.arange(tm)[None, :]).astype(MXU_DTYPE)
    rt = [pad_cols(p["router_group_w"]), pad_row(p["router_group_b"]),
          pad_cols(p["router_expert_w"]), pad_row(p["router_expert_b"]), tril]
    sds = jax.ShapeDtypeStruct
    return pl.pallas_call(
        functools.partial(_mix_kernel, tm=tm, halo=halo),
        grid=(B, S // tm),
        in_specs=[row(D), colT, colT, row(CONV_CH),
                  pl.BlockSpec((1, halo, CONV_CH), lambda b, i: (b, jnp.maximum(i * (tm // halo) - 1, 0), 0))]
                 + [full(a) for a in small] + [vec] + [full(a) for a in tail] + [vec, vec] + [full(a) for a in rt],
        out_specs=[row(D), row(D), row(8), row(8), pl.BlockSpec((1, LANES), lambda b, i: (0, 0))],
        out_shape=[sds((B, S, D), F32), sds((B, S, D), F32), sds((B, S, 8), I32), sds((B, S, 8), F32),
                   sds((1, LANES), F32)],
        scratch_shapes=[pltpu.VMEM((halo + 2 * tm, CONV_CH), F32), pltpu.VMEM((1, LANES), F32)],
        compiler_params=_params(("arbitrary", "arbitrary")),
        name="mix_outproj_route",
    )(x, ymT, ydT, u, u, *small, g1, *tail, sc2, sh2, *rt)


ROW_TILE = 512
DMA_UNROLL = 8


def _dispatch_kernel(dest_ref, h_ref, xs_init_ref, xs_ref, sem, *, tm):
    del xs_init_ref
    base = pl.program_id(0) * tm

    def issue(t, _):
        for k in range(2):
            d = dest_ref[2 * (base + t) + k]
            pltpu.make_async_copy(h_ref.at[pl.ds(t, 1)], xs_ref.at[pl.ds(d, 1)], sem).start()
        return 0
    lax.fori_loop(0, tm, issue, 0, unroll=DMA_UNROLL)
    for k in range(2):
        pltpu.make_async_copy(h_ref, xs_ref.at[pl.ds(0, tm)], sem).wait()


def _ffn_kernel(blk_e_ref, nvalid_ref, x_ref, w1_ref, w3_ref, w2_ref, y_ref, w1b, w3b, w2b):
    i = pl.program_id(0)

    @pl.when(jnp.logical_or(i == 0, blk_e_ref[i] != blk_e_ref[jnp.maximum(i - 1, 0)]))
    def _():
        w1b[...] = w1_ref[0].astype(MXU_DTYPE)
        w3b[...] = w3_ref[0].astype(MXU_DTYPE)
        w2b[...] = w2_ref[0].astype(MXU_DTYPE)

    @pl.when(nvalid_ref[i] > 0)
    def _():
        xb = x_ref[...].astype(MXU_DTYPE)
        a = _dot(xb, w1b[...])
        hid = (a * jax.nn.sigmoid(a)) * _dot(xb, w3b[...])
        y_ref[...] = _dot(hid.astype(MXU_DTYPE), w2b[...])

    @pl.when(nvalid_ref[i] == 0)
    def _():
        y_ref[...] = jnp.zeros_like(y_ref)


def _combine_kernel(dest_ref, y_hbm, x_ref, gate_ref, g2_ref, fg_ref, o_ref, ybuf, sem, *, tm, final):
    base = pl.program_id(0) * tm

    def issue(t, _):
        for k in range(2):
            d = dest_ref[2 * (base + t) + k]
            pltpu.make_async_copy(y_hbm.at[pl.ds(d, 1)], ybuf.at[k, pl.ds(t, 1)], sem).start()
        return 0
    lax.fori_loop(0, tm, issue, 0, unroll=DMA_UNROLL)
    for k in range(2):
        pltpu.make_async_copy(y_hbm.at[pl.ds(0, tm)], ybuf.at[k], sem).wait()
    g = gate_ref[...]
    x = x_ref[...] + g2_ref[0] * (g[:, 0:1] * ybuf[0] + g[:, 1:2] * ybuf[1])
    if final:
        x = _rms(x) * fg_ref[...]
    o_ref[...] = x


def _moe_ffn(h2, slot, counts, w1, w3, w2):
    T, D = h2.shape
    P = 2 * T + N_EXPERTS * MOE_BLOCK
    nblk = P // MOE_BLOCK
    tm = min(ROW_TILE, T)
    counts = counts[0, :N_EXPERTS].astype(I32)
    padded = (counts + MOE_BLOCK - 1) // MOE_BLOCK * MOE_BLOCK
    pad_end = jnp.cumsum(padded)
    pad_start = pad_end - padded
    dest = (pad_start[slot[:, 0:2]] + slot[:, 2:4]).reshape(-1).astype(I32)
    blk_start = jnp.arange(nblk, dtype=I32) * MOE_BLOCK
    blk_e = jnp.minimum(jnp.searchsorted(pad_end, blk_start, side="right"), N_EXPERTS - 1).astype(I32)
    nvalid = jnp.clip((pad_start + counts)[blk_e] - blk_start, 0, MOE_BLOCK).astype(I32)

    xs = pl.pallas_call(
        functools.partial(_dispatch_kernel, tm=tm),
        grid_spec=pltpu.PrefetchScalarGridSpec(
            num_scalar_prefetch=1, grid=(T // tm,),
            in_specs=[pl.BlockSpec((tm, D), lambda i, d: (i, 0)), pl.BlockSpec(memory_space=pl.ANY)],
            out_specs=pl.BlockSpec(memory_space=pl.ANY),
            scratch_shapes=[pltpu.SemaphoreType.DMA]),
        out_shape=jax.ShapeDtypeStruct((P, D), F32),
        input_output_aliases={2: 0},
        compiler_params=_params(("arbitrary",)),
        name="moe_dispatch",
    )(dest, h2, jnp.zeros((P, D), F32))

    ys = pl.pallas_call(
        _ffn_kernel,
        grid_spec=pltpu.PrefetchScalarGridSpec(
            num_scalar_prefetch=2, grid=(nblk,),
            in_specs=[pl.BlockSpec((MOE_BLOCK, D), lambda i, be, nv: (i, 0)),
                      pl.BlockSpec((1, D, D_EXPERT), lambda i, be, nv: (be[i], 0, 0)),
                      pl.BlockSpec((1, D, D_EXPERT), lambda i, be, nv: (be[i], 0, 0)),
                      pl.BlockSpec((1, D_EXPERT, D), lambda i, be, nv: (be[i], 0, 0))],
            out_specs=pl.BlockSpec((MOE_BLOCK, D), lambda i, be, nv: (i, 0)),
            scratch_shapes=[pltpu.VMEM((D, D_EXPERT), MXU_DTYPE), pltpu.VMEM((D, D_EXPERT), MXU_DTYPE),
                            pltpu.VMEM((D_EXPERT, D), MXU_DTYPE)]),
        out_shape=jax.ShapeDtypeStruct((P, D), F32),
        compiler_params=_params(("arbitrary",)),
        name="moe_ffn",
    )(blk_e, nvalid, xs, w1, w3, w2)
    return ys, dest


def _moe_combine(x1, ys, dest, gates, g2, final_g):
    B, S, D = x1.shape
    T = B * S
    tm = min(ROW_TILE, S)
    final = final_g is not None
    fg = (final_g if final else jnp.ones((D,), F32)).reshape(1, D)
    out = pl.pallas_call(
        functools.partial(_combine_kernel, tm=tm, final=final),
        grid_spec=pltpu.PrefetchScalarGridSpec(
            num_scalar_prefetch=1, grid=(T // tm,),
            in_specs=[pl.BlockSpec(memory_space=pl.ANY),
                      pl.BlockSpec((tm, D), lambda i, d: (i, 0)),
                      pl.BlockSpec((tm, 8), lambda i, d: (i, 0)),
                      pl.BlockSpec((1, 1, D), lambda i, d: (i // (S // tm), 0, 0)),
                      pl.BlockSpec((1, D), lambda i, d: (0, 0))],
            out_specs=pl.BlockSpec((tm, D), lambda i, d: (i, 0)),
            scratch_shapes=[pltpu.VMEM((2, tm, D), F32), pltpu.SemaphoreType.DMA]),
        out_shape=jax.ShapeDtypeStruct((T, D), F32),
        compiler_params=_params(("arbitrary",)),
        name="moe_combine",
    )(dest, ys, x1.reshape(T, D), gates.reshape(T, 8), g2, fg)
    return out.reshape(B, S, D)


def kernel(x, c, ada_w, ada_b, norm1_g, w_in, conv_w, conv_b, conv_ln_g, conv_ln_b, moba_norm_g, dsa_norm_g, w_out, norm2_g, router_group_w, router_group_b, router_expert_w, router_expert_b, expert_w1, expert_w3, expert_w2, final_g):
    B, S, D = x.shape
    L = ada_w.shape[0]
    mod = _modulation(c, ada_w, ada_b)
    for l in range(L):
        sh1, sc1, g1, sh2, sc2, g2 = [m[:, None, :] for m in jnp.split(mod[l], 6, axis=-1)]
        qmT, km, vmT, kmean, u, qdT, kd, vdT, qiT, ki, wT = _inproj(
            x, norm1_g[l], sc1, sh1, _split_w_in(w_in[l]))
        y_moba = _moba(qmT, km, vmT, kmean)
        y_dsa = _dsa(qdT, kd, vdT, qiT, ki, wT)
        p = dict(conv_w=conv_w[l], conv_b=conv_b[l], conv_ln_g=conv_ln_g[l], conv_ln_b=conv_ln_b[l],
                 moba_norm_g=moba_norm_g[l], dsa_norm_g=dsa_norm_g[l], w_out=w_out[l], norm2_g=norm2_g[l],
                 router_group_w=router_group_w[l], router_group_b=router_group_b[l],
                 router_expert_w=router_expert_w[l], router_expert_b=router_expert_b[l])
        x1, h2, slot, gates, counts = _mix(x, y_moba, y_dsa, u, p, g1, sc2, sh2)
        ys, dest = _moe_ffn(h2.reshape(B * S, D), slot.reshape(B * S, 8), counts,
                            expert_w1[l], expert_w3[l], expert_w2[l])
        x = _moe_combine(x1, ys, dest, gates, g2, final_g if l == L - 1 else None)
    return x
```

```python
import functools

import jax
import jax.numpy as jnp
from jax import lax
from jax.experimental import pallas as pl
from jax.experimental.pallas import tpu as pltpu

F32 = jnp.float32
I32 = jnp.int32
I16 = jnp.int16
MXU_DTYPE = jnp.bfloat16

HEAD_DIM = 64
MOBA_HEADS = 6
MOBA_BLOCK = 256
MOBA_TOPK = 3
CONV_CH = 256
CONV_WIDTH = 31
DSA_HEADS = 6
DSA_TOPK = 256
IDX_HEADS = 8
IDX_DIM = 64
ATT_W = 384
N_GROUPS = 4
EXPERTS_PER_GROUP = 8
N_EXPERTS = 32
D_EXPERT = 512
MOE_BLOCK = 256
EPS = 1e-6

LANES = 128
PACK_ROWS = 16
PAIR_W = 2 * HEAD_DIM
V_ROWS = HEAD_DIM + PACK_ROWS
VT_W = 6 * V_ROWS
NEG = -1e30
INT_MIN = -2 ** 31
VMEM_LIMIT = 56 * 1024 * 1024
LOG2_E = 1.4426950408889634
Q_SCALE = HEAD_DIM ** -0.5 * LOG2_E
IDX_SCALE = IDX_HEADS ** -0.5 * IDX_DIM ** -0.5


def _params(sem):
    return pltpu.CompilerParams(dimension_semantics=sem, vmem_limit_bytes=VMEM_LIMIT)


def _rms(xf):
    return xf * lax.rsqrt(jnp.mean(xf * xf, axis=-1, keepdims=True) + EPS)


def _dot(a, b):
    return jnp.dot(a, b, preferred_element_type=F32)


def _dot_nt(a, b):
    return lax.dot_general(a, b, (((1,), (1,)), ((), ())), preferred_element_type=F32)


def _resident(block_shape, index_map):
    return pl.BlockSpec(block_shape, index_map, pipeline_mode=pl.Buffered(1))


def _mod_kernel(c_ref, w_ref, b_ref, o_ref):
    c = c_ref[...]
    o_ref[0] = _dot(c * jax.nn.sigmoid(c), w_ref[0]) + b_ref[0]


def _modulation(c, ada_w, ada_b):
    L, D, N = ada_w.shape
    B = c.shape[0]
    rows = 8
    cp = jnp.zeros((rows, D), F32).at[:B].set(c)
    tn = 1536
    out = pl.pallas_call(
        _mod_kernel,
        grid=(L, N // tn),
        in_specs=[pl.BlockSpec((rows, D), lambda l, j: (0, 0)),
                  pl.BlockSpec((1, D, tn), lambda l, j: (l, 0, j)),
                  pl.BlockSpec((1, 1, tn), lambda l, j: (l, 0, j))],
        out_specs=pl.BlockSpec((1, rows, tn), lambda l, j: (l, 0, j)),
        out_shape=jax.ShapeDtypeStruct((L, rows, N), F32),
        compiler_params=_params(("arbitrary", "arbitrary")),
        name="adaln_mod",
    )(cp, ada_w, ada_b.reshape(L, 1, N))
    return out[:, :B]


def _store_values_with_ones(vT_ref, vT, tm):
    ones = jnp.ones((PACK_ROWS, tm), vT_ref.dtype)
    for h in range(ATT_W // HEAD_DIM):
        vT_ref[0, h * V_ROWS:h * V_ROWS + HEAD_DIM, :] = vT[h * HEAD_DIM:(h + 1) * HEAD_DIM].astype(vT_ref.dtype)
        vT_ref[0, h * V_ROWS + HEAD_DIM:(h + 1) * V_ROWS, :] = ones


def _inproj_kernel(x_ref, g_ref, sc_ref, sh_ref, wqmT, wkm, wvmT, wglu, wqdT, wkd, wvdT, wqiT, wki, wwiT,
                   qmT_ref, km_ref, vmT_ref, kmean_ref, u_ref, qdT_ref, kd_ref, vdT_ref, qiT_ref, ki_ref, wiT_ref,
                   *, tm):
    h = _rms(x_ref[0]) * g_ref[...] * (1.0 + sc_ref[0]) + sh_ref[0]
    hb = h.astype(MXU_DTYPE)

    qmT_ref[0] = _dot_nt(wqmT[...], hb).astype(qmT_ref.dtype)
    k_rows = _dot(hb, wkm[...])
    km_ref[0] = k_rows.astype(km_ref.dtype)
    for c in range(tm // MOBA_BLOCK):
        kmean_ref[0, c] = jnp.mean(k_rows[c * MOBA_BLOCK:(c + 1) * MOBA_BLOCK], axis=0, keepdims=True)
    _store_values_with_ones(vmT_ref, _dot_nt(wvmT[...], hb), tm)
    glu = _dot(hb, wglu[...])
    u_ref[0] = glu[:, :CONV_CH] * jax.nn.sigmoid(glu[:, CONV_CH:])
    qdT_ref[0] = _dot_nt(wqdT[...], hb).astype(qdT_ref.dtype)
    kd_ref[0] = _dot(hb, wkd[...]).astype(kd_ref.dtype)
    _store_values_with_ones(vdT_ref, _dot_nt(wvdT[...], hb), tm)
    qiT_ref[0] = _dot_nt(wqiT[...], hb).astype(qiT_ref.dtype)
    ki_ref[0] = _dot(hb, wki[...]).astype(ki_ref.dtype)
    wiT_ref[0] = _dot_nt(wwiT[...], hb) * IDX_SCALE


_W_NAMES = ["wqmT", "wkm", "wvmT", "wglu", "wqdT", "wkd", "wvdT", "wqiT", "wki", "wwiT"]


def _split_w_in(w_in):
    offs = [0, 384, 768, 1152, 1408, 1664, 2048, 2432, 2816, 3328, 3392, 3400]
    piece = lambda k: w_in[:, offs[k]:offs[k + 1]]
    c = lambda w: w.astype(MXU_DTYPE)
    return dict(
        wqmT=c(piece(0).T * Q_SCALE), wkm=c(piece(1)), wvmT=c(piece(2).T),
        wglu=c(w_in[:, offs[3]:offs[5]]),
        wqdT=c(piece(5).T * Q_SCALE), wkd=c(piece(6)), wvdT=c(piece(7).T),
        wqiT=c(piece(8).T), wki=c(piece(9)), wwiT=c(piece(10).T))


def _inproj(x, n1g, sc1, sh1, w):
    B, S, D = x.shape
    tm = 512
    nb_t = tm // MOBA_BLOCK
    row = lambda width: pl.BlockSpec((1, tm, width), lambda b, i: (b, i, 0))
    col = lambda height: pl.BlockSpec((1, height, tm), lambda b, i: (b, 0, i))
    vec = pl.BlockSpec((1, 1, D), lambda b, i: (b, 0, 0))
    full = lambda a: pl.BlockSpec(a.shape, lambda b, i: (0,) * a.ndim)

    ins = [x, n1g.reshape(1, D), sc1, sh1] + [w[k] for k in _W_NAMES]
    in_specs = [row(D), pl.BlockSpec((1, D), lambda b, i: (0, 0)), vec, vec] + [full(w[k]) for k in _W_NAMES]

    sds = jax.ShapeDtypeStruct
    QI_W = IDX_HEADS * IDX_DIM
    out_shape = [
        sds((B, ATT_W, S), MXU_DTYPE), sds((B, S, ATT_W), MXU_DTYPE), sds((B, VT_W, S), MXU_DTYPE),
        sds((B, S // MOBA_BLOCK, 1, ATT_W), F32),
        sds((B, S, CONV_CH), F32),
        sds((B, ATT_W, S), MXU_DTYPE), sds((B, S, ATT_W), MXU_DTYPE), sds((B, VT_W, S), MXU_DTYPE),
        sds((B, QI_W, S), MXU_DTYPE), sds((B, S, IDX_DIM), MXU_DTYPE), sds((B, IDX_HEADS, S), F32)]
    out_specs = [
        col(ATT_W), row(ATT_W), col(VT_W),
        pl.BlockSpec((1, nb_t, 1, ATT_W), lambda b, i: (b, i, 0, 0)),
        row(CONV_CH),
        col(ATT_W), row(ATT_W), col(VT_W),
        col(QI_W), row(IDX_DIM), col(IDX_HEADS)]

    return pl.pallas_call(
        functools.partial(_inproj_kernel, tm=tm),
        grid=(B, S // tm),
        in_specs=in_specs, out_specs=out_specs, out_shape=out_shape,
        compiler_params=_params(("arbitrary", "arbitrary")),
        name="inproj",
    )(*ins)


def _pair_queries(qT_ref, p, TQ):
    first = lax.broadcasted_iota(I32, (PAIR_W, TQ), 0) < HEAD_DIM
    qpair = qT_ref[0, p * PAIR_W:(p + 1) * PAIR_W, :]
    zero = jnp.zeros_like(qpair)
    return jnp.where(first, qpair, zero), jnp.where(first, zero, qpair)


def _fold8(x, op):
    return op(x.reshape(x.shape[0] // 8, 8, x.shape[1]), axis=0)


def _init_weights_scratch(p_ref):
    @pl.when(jnp.logical_and(pl.program_id(0) == 0, pl.program_id(1) == 0))
    def _():
        p_ref[...] = jnp.zeros_like(p_ref)


def _pair_attention(k_ref, vT_ref, p, qpads, n_tiles, bias_fn, s_ref, p_ref, *, TK, RC, TQ):
    n_chunks = TK // RC
    ps = slice(p * PAIR_W, (p + 1) * PAIR_W)
    vrows = [slice((2 * p + r) * V_ROWS, (2 * p + r + 1) * V_ROWS) for r in range(2)]

    def scores(ks, c, r):
        rows = pl.ds(ks + c * RC, RC)
        return _dot(k_ref[0, rows, ps], qpads[r]) + bias_fn(r, ks, c)

    def pv(r, ks, alpha, acc, live=None):
        w = p_ref[r]
        if live is not None:
            w = jnp.where(live, w, jnp.zeros_like(w))
        return alpha * acc + _dot(vT_ref[0, vrows[r], pl.ds(pl.multiple_of(ks, TK), TK)], w)

    def section(ks_next, r_next, r_cur, m_new, deferred):
        runmax = jnp.full((8, TQ), NEG, F32)
        out = None
        for c in range(n_chunks):
            cs = slice(c * RC, (c + 1) * RC)
            s = scores(ks_next, c, r_next)
            s_ref[r_next, cs, :] = s
            runmax = jnp.maximum(runmax, _fold8(s, jnp.max))
            p_ref[r_cur, cs, :] = jnp.exp2(s_ref[r_cur, cs, :] - m_new).astype(p_ref.dtype)
            if c == 0:
                out = deferred()
        return jnp.max(runmax, axis=0, keepdims=True), out

    runmax = jnp.full((8, TQ), NEG, F32)
    for c in range(n_chunks):
        s = scores(0, c, 0)
        s_ref[0, c * RC:(c + 1) * RC, :] = s
        runmax = jnp.maximum(runmax, _fold8(s, jnp.max))
    tmax0 = jnp.max(runmax, axis=0, keepdims=True)

    def tile(j, carry):
        m0, a0, m1, a1, tmax, alpha1, ks1 = carry
        ks = pl.multiple_of(j * TK, TK)
        ks_next = pl.multiple_of(jnp.minimum(j + 1, n_tiles - 1) * TK, TK)
        m0n = jnp.maximum(m0, tmax)
        tmax1, a1 = section(ks, 1, 0, m0n, lambda: pv(1, ks1, alpha1, a1, live=j > 0))
        alpha0 = jnp.exp2(m0 - m0n)
        m1n = jnp.maximum(m1, tmax1)
        tmax0n, a0 = section(ks_next, 0, 1, m1n, lambda: pv(0, ks, alpha0, a0))
        return m0n, a0, m1n, a1, tmax0n, jnp.exp2(m1 - m1n), ks

    m_init, a_init = jnp.full((1, TQ), NEG, F32), jnp.zeros((V_ROWS, TQ), F32)
    _, a0, _, a1, _, alpha1, ks1 = lax.fori_loop(
        0, n_tiles, tile, (m_init, a_init, m_init, a_init, tmax0, jnp.ones((1, TQ), F32), jnp.int32(0)))
    a1 = pv(1, ks1, alpha1, a1)
    return [a[0:HEAD_DIM] / a[HEAD_DIM:HEAD_DIM + 1] for a in (a0, a1)]


def _moba_kernel(qT_ref, k_ref, vT_ref, km_ref, o_ref, sb_ref, cb_ref, s_ref, p_ref, *, nb, TK, RC):
    TQ = MOBA_BLOCK
    i = pl.program_id(1)
    _init_weights_scratch(p_ref)
    blk = lax.broadcasted_iota(I32, (nb, TQ), 0)
    past = blk < i
    cb_ref[...] = jnp.where(lax.broadcasted_iota(I32, (TQ, TQ), 0) <= lax.broadcasted_iota(I32, (TQ, TQ), 1),
                            0.0, NEG)
    blocks_per_tile = TK // TQ
    n_tiles = (i + blocks_per_tile) // blocks_per_tile

    for p in range(MOBA_HEADS // 2):
        ps = slice(p * PAIR_W, (p + 1) * PAIR_W)
        hrows = [slice((2 * p + r) * HEAD_DIM, (2 * p + r + 1) * HEAD_DIM) for r in range(2)]
        qpads = _pair_queries(qT_ref, p, TQ)
        for r in range(2):
            g = jnp.where(past, _dot(km_ref[0, :, ps], qpads[r]), -jnp.inf)
            sel = jnp.zeros((nb, TQ), I32)
            for _ in range(MOBA_TOPK):
                mx = jnp.max(g, axis=0, keepdims=True)
                first = jnp.min(jnp.where(g == mx, blk, nb), axis=0, keepdims=True)
                hit = blk == first
                sel = jnp.where(hit, 1, sel)
                g = jnp.where(hit, -jnp.inf, g)
            sb_ref[r] = jnp.where(past, jnp.where(sel > 0, 0.0, NEG), NEG)

        def bias_fn(r, ks, c):
            b = ks // TQ + (c * RC) // TQ
            within = (c * RC) % TQ
            row = jnp.broadcast_to(sb_ref[r, pl.ds(b, 1), :], (RC, TQ))
            return jnp.where(b == i, cb_ref[within:within + RC, :], row)

        outs = _pair_attention(k_ref, vT_ref, p, qpads, n_tiles, bias_fn, s_ref, p_ref,
                               TK=TK, RC=RC, TQ=TQ)
        for r in range(2):
            o_ref[0, hrows[r], :] = outs[r]


def _moba(qT, k, vT, kmean):
    B, W, S = qT.shape
    nb = S // MOBA_BLOCK
    TK, RC = min(1024, S), 128
    assert S % TK == 0 and TK % MOBA_BLOCK == 0 and MOBA_BLOCK % RC == 0
    km = kmean.reshape(B, nb, W).astype(MXU_DTYPE)
    return pl.pallas_call(
        functools.partial(_moba_kernel, nb=nb, TK=TK, RC=RC),
        grid=(B, nb),
        in_specs=[pl.BlockSpec((1, W, MOBA_BLOCK), lambda b, i: (b, 0, i)),
                  _resident((1, S, W), lambda b, i: (b, 0, 0)),
                  _resident((1, vT.shape[1], S), lambda b, i: (b, 0, 0)),
                  _resident((1, nb, W), lambda b, i: (b, 0, 0))],
        out_specs=pl.BlockSpec((1, W, MOBA_BLOCK), lambda b, i: (b, 0, i)),
        out_shape=jax.ShapeDtypeStruct((B, W, S), F32),
        scratch_shapes=[pltpu.VMEM((2, nb, MOBA_BLOCK), F32), pltpu.VMEM((MOBA_BLOCK, MOBA_BLOCK), F32),
                        pltpu.VMEM((2, TK, MOBA_BLOCK), F32), pltpu.VMEM((2, TK, MOBA_BLOCK), MXU_DTYPE)],
        compiler_params=_params(("arbitrary", "arbitrary")),
        name="moba",
    )(qT, k, vT, km)


def _order_key(x):
    b = pltpu.bitcast(x, I32)
    return b ^ ((b >> 31) & 0x7FFFFFFF)


def _dsa_kernel(qT_ref, k_ref, vT_ref, qiT_ref, ki_ref, w_ref, o_ref, hi_ref, lo_ref, bias_ref, s_ref, p_ref,
                *, TQ, TK, CH, TKA, RC, topk):
    i = pl.program_id(1)
    _init_weights_scratch(p_ref)
    n_t = ((i + 1) * TQ + TK - 1) // TK
    krow = lax.broadcasted_iota(I32, (TK, TQ), 0)
    qcol = i * TQ + lax.broadcasted_iota(I32, (TK, TQ), 1)

    def index_tile(j, _):
        ks = pl.ds(pl.multiple_of(j * TK, TK), TK)
        ki = ki_ref[0, ks, :]
        acc = jnp.zeros((TK, TQ), F32)
        for h in range(IDX_HEADS):
            s = _dot(ki, qiT_ref[0, h * IDX_DIM:(h + 1) * IDX_DIM, :])
            acc = acc + w_ref[0, h:h + 1, :] * jnp.maximum(s, 0.0)
        key = jnp.where(j * TK + krow <= qcol, _order_key(acc), INT_MIN)
        hi_ref[ks, :] = (key >> 16).astype(I16)
        lo_ref[ks, :] = ((key & 0xFFFF) - 32768).astype(I16)
        return 0
    lax.fori_loop(0, n_t, index_tile, 0)

    n_ch = n_t * (TK // CH)
    one, zero = jnp.ones((PACK_ROWS, TQ), I16), jnp.zeros((PACK_ROWS, TQ), I16)
    rows16 = lax.broadcasted_iota(I32, (PACK_ROWS, TQ), 0)

    def rows_to_i16(v):
        return jnp.broadcast_to(v, (PACK_ROWS, TQ)).astype(I16)

    n_sub = CH // PACK_ROWS

    def chunk_blocks(refs, c):
        r0 = pl.multiple_of(c * CH, CH)
        vals = [ref[pl.ds(r0, CH), :] for ref in refs]
        return [([v[u * PACK_ROWS:(u + 1) * PACK_ROWS] for v in vals], r0 + u * PACK_ROWS) for u in range(n_sub)]

    def count(refs, flag):
        def chunk(c, parts):
            parts = list(parts)
            for u, (blocks, r) in enumerate(chunk_blocks(refs, c)):
                parts[u % len(parts)] = parts[u % len(parts)] + flag(*blocks, r)
            return tuple(parts)
        parts = lax.fori_loop(0, n_ch, chunk, (zero,) * 4)
        part = (parts[0] + parts[1]) + (parts[2] + parts[3])
        return jnp.sum(part.astype(I32), axis=0, keepdims=True)

    def search16(ref, k_needed):
        def step(t, ans_u):
            cand_u = ans_u | lax.shift_left(jnp.int32(1), 15 - t)
            c16 = rows_to_i16(cand_u - 32768)
            cnt = count([ref], lambda x, r: jnp.where(x >= c16, one, zero))
            return jnp.where(cnt >= k_needed, cand_u, ans_u)
        return lax.fori_loop(0, 16, step, jnp.zeros((1, TQ), I32)) - 32768

    P = search16(hi_ref, topk)
    P = jnp.maximum(P, -32767)
    P16 = rows_to_i16(P)
    n_hi_gt = count([hi_ref], lambda hi, r: jnp.where(hi > P16, one, zero))

    lowest = jnp.full((PACK_ROWS, TQ), -32768, I16)

    def restrict_lo(c, _):
        out = [jnp.where(hi == P16, lo, lowest) for (hi, lo), _ in chunk_blocks([hi_ref, lo_ref], c)]
        lo_ref[pl.ds(pl.multiple_of(c * CH, CH), CH), :] = jnp.concatenate(out, axis=0)
        return 0
    lax.fori_loop(0, n_ch, restrict_lo, 0)
    Q = search16(lo_ref, topk - n_hi_gt)
    Q16 = rows_to_i16(Q)

    def in_bucket_eq(hi, lo):
        return jnp.where(hi == P16, jnp.where(lo == Q16, one, zero), zero)
    n_gt = n_hi_gt + count([lo_ref], lambda lo, r: jnp.where(lo > Q16, one, zero))
    n_eq = count([hi_ref, lo_ref], lambda hi, lo, r: in_bucket_eq(hi, lo))
    need = topk - n_gt
    tied = n_eq > need
    any_tied = jnp.max(jnp.where(tied, 1, 0)) > 0

    def last_tie_row(_):
        def step(t, ans):
            cand = ans | lax.shift_left(jnp.int32(1), 14 - t)
            c16 = rows_to_i16(cand)
            cnt = count([hi_ref, lo_ref],
                        lambda hi, lo, r: jnp.where((r + rows16).astype(I16) < c16, in_bucket_eq(hi, lo), zero))
            return jnp.where(cnt < need, cand, ans)
        return lax.fori_loop(0, 15, step, jnp.zeros((1, TQ), I32))
    jmax = lax.cond(any_tied, last_tie_row, lambda _: jnp.full((1, TQ), 32767, I32), 0)
    J16 = rows_to_i16(jnp.where(tied, jmax, 32767))

    def to_bias(c, _):
        out = []
        for (hi, lo), r in chunk_blocks([hi_ref, lo_ref], c):
            tie_keep = jnp.where((r + rows16).astype(I16) <= J16, one, zero)
            in_p = jnp.where(lo > Q16, one, jnp.where(lo == Q16, tie_keep, zero))
            keep = jnp.where(hi > P16, one, jnp.where(hi == P16, in_p, zero))
            out.append(jnp.where(keep.astype(I32) > 0, 0.0, NEG))
        bias_ref[pl.ds(pl.multiple_of(c * CH, CH), CH), :] = jnp.concatenate(out, axis=0)
        return 0
    lax.fori_loop(0, n_ch, to_bias, 0)

    tiles_per_att = TKA // TK
    n_att = (n_t + tiles_per_att - 1) // tiles_per_att
    for extra in range(1, tiles_per_att):
        @pl.when(n_t % tiles_per_att == extra)
        def _():
            for t in range(tiles_per_att - extra):
                bias_ref[pl.ds(pl.multiple_of((n_t + t) * TK, TK), TK), :] = jnp.full((TK, TQ), NEG, F32)

    for p in range(DSA_HEADS // 2):
        hrows = [slice((2 * p + r) * HEAD_DIM, (2 * p + r + 1) * HEAD_DIM) for r in range(2)]
        qpads = _pair_queries(qT_ref, p, TQ)
        outs = _pair_attention(k_ref, vT_ref, p, qpads, n_att,
                               lambda r, ks, c: bias_ref[pl.ds(ks + c * RC, RC), :], s_ref, p_ref,
                               TK=TKA, RC=RC, TQ=TQ)
        for r in range(2):
            o_ref[0, hrows[r], :] = outs[r]


def _dsa(qT, k, vT, qiT, ki, wT):
    B, W, S = qT.shape
    assert S < 2 ** 15, "key indices are compared as int16"
    topk = min(DSA_TOPK, S // 4)
    TQ, TK, CH = 256, min(512, S), 256
    TKA, RC = min(1024, S), 128
    assert S % TKA == 0 and TKA % TK == 0
    return pl.pallas_call(
        functools.partial(_dsa_kernel, TQ=TQ, TK=TK, CH=CH, TKA=TKA, RC=RC, topk=topk),
        grid=(B, S // TQ),
        in_specs=[pl.BlockSpec((1, W, TQ), lambda b, i: (b, 0, i)),
                  _resident((1, S, W), lambda b, i: (b, 0, 0)),
                  _resident((1, vT.shape[1], S), lambda b, i: (b, 0, 0)),
                  pl.BlockSpec((1, IDX_HEADS * IDX_DIM, TQ), lambda b, i: (b, 0, i)),
                  _resident((1, S, IDX_DIM), lambda b, i: (b, 0, 0)),
                  pl.BlockSpec((1, IDX_HEADS, TQ), lambda b, i: (b, 0, i))],
        out_specs=pl.BlockSpec((1, W, TQ), lambda b, i: (b, 0, i)),
        out_shape=jax.ShapeDtypeStruct((B, W, S), F32),
        scratch_shapes=[pltpu.VMEM((S, TQ), I16), pltpu.VMEM((S, TQ), I16), pltpu.VMEM((S, TQ), F32),
                        pltpu.VMEM((2, TKA, TQ), F32), pltpu.VMEM((2, TKA, TQ), MXU_DTYPE)],
        compiler_params=_params(("arbitrary", "arbitrary")),
        name="dsa",
    )(qT, k, vT, qiT, ki, wT)


def _mix_kernel(x_ref, ym_ref, yd_ref, u_ref, up_ref, cw_ref, cb_ref, lg_ref, lb_ref, mg_ref, dg_ref,
                wom_ref, woc_ref, wod_ref, g1_ref, n2_ref, sc_ref, sh_ref, rgw_ref, rgb_ref, rew_ref, reb_ref,
                tril_ref, x1_ref, h2_ref, slot_ref, gate_ref, counts_ref, win_ref, cnt_ref, *, tm, halo):
    i = pl.program_id(1)
    win_ref[0:halo, :] = jnp.where(i > 0, up_ref[0], 0.0)
    win_ref[halo:halo + tm, :] = u_ref[0]
    rc = 128
    off = halo - (CONV_WIDTH - 1)
    cw = cw_ref[...]
    for r0 in range(0, tm, rc):
        acc = jnp.zeros((rc, CONV_CH), F32) + cb_ref[...]
        for k in range(CONV_WIDTH):
            acc = acc + cw[k:k + 1, :] * win_ref[r0 + off + k:r0 + off + k + rc, :]
        mu = jnp.mean(acc, axis=-1, keepdims=True)
        xc = acc - mu
        yn = xc * lax.rsqrt(jnp.mean(xc * xc, axis=-1, keepdims=True) + EPS) * lg_ref[...] + lb_ref[...]
        win_ref[halo + tm + r0:halo + tm + r0 + rc, :] = yn * jax.nn.sigmoid(yn)
    yc = win_ref[halo + tm:halo + 2 * tm, :]

    ymn = (_rms(ym_ref[0].T) * mg_ref[...]).astype(MXU_DTYPE)
    ydn = (_rms(yd_ref[0].T) * dg_ref[...]).astype(MXU_DTYPE)
    proj = _dot(ymn, wom_ref[...]) + _dot(yc.astype(MXU_DTYPE), woc_ref[...]) + _dot(ydn, wod_ref[...])
    x1 = x_ref[0] + g1_ref[0] * proj
    x1_ref[0] = x1
    h2 = _rms(x1) * n2_ref[...] * (1.0 + sc_ref[0]) + sh_ref[0]
    h2_ref[0] = h2

    lane = lax.broadcasted_iota(I32, (tm, LANES), 1)
    gl = _dot(h2, rgw_ref[...]) + rgb_ref[...]
    gl = jnp.where(lane < N_GROUPS, gl, -jnp.inf)
    gmax = jnp.max(gl, axis=-1, keepdims=True)
    g_idx = jnp.min(jnp.where(gl == gmax, lane, LANES), axis=-1, keepdims=True)
    g_w = 1.0 / jnp.sum(jnp.exp(gl - gmax), axis=-1, keepdims=True)
    el = _dot(h2, rew_ref[...]) + reb_ref[...]
    in_group = jnp.logical_and(lane >= g_idx * EXPERTS_PER_GROUP, lane < (g_idx + 1) * EXPERTS_PER_GROUP)
    el = jnp.where(in_group, el, -jnp.inf)
    t1 = jnp.max(el, axis=-1, keepdims=True)
    e1 = jnp.min(jnp.where(el == t1, lane, LANES), axis=-1, keepdims=True)
    el2 = jnp.where(lane == e1, -jnp.inf, el)
    t2 = jnp.max(el2, axis=-1, keepdims=True)
    e2 = jnp.min(jnp.where(el2 == t2, lane, LANES), axis=-1, keepdims=True)
    r = jnp.exp(t2 - t1)
    p1 = 1.0 / (1.0 + r)
    lane8 = lax.broadcasted_iota(I32, (tm, 8), 1)
    gate_ref[0] = jnp.where(lane8 == 0, g_w * p1, jnp.where(lane8 == 1, g_w * p1 * r, 0.0))

    @pl.when(jnp.logical_and(pl.program_id(0) == 0, i == 0))
    def _():
        cnt_ref[...] = jnp.zeros_like(cnt_ref)
    cnt = cnt_ref[...]
    ranks = []
    for e_k in (e1, e2):
        onehot = jnp.where(lane == e_k, 1.0, 0.0)
        before = cnt + _dot(tril_ref[...], onehot.astype(MXU_DTYPE))
        ranks.append(jnp.sum(onehot * before, axis=-1, keepdims=True).astype(I32))
        cnt = cnt + jnp.sum(onehot, axis=0, keepdims=True)
    cnt_ref[...] = cnt
    counts_ref[...] = cnt
    cols = jnp.where(lane == 0, e1, jnp.where(lane == 1, e2, jnp.where(
        lane == 2, ranks[0], jnp.where(lane == 3, ranks[1], 0)))).astype(F32)
    slot_ref[...] = cols.T[0:8, :]


def _mix(x, ymT, ydT, u, p, g1, sc2, sh2):
    B, S, D = x.shape
    tm, halo = 512, 32
    row = lambda width: pl.BlockSpec((1, tm, width), lambda b, i: (b, i, 0))
    colT = pl.BlockSpec((1, ATT_W, tm), lambda b, i: (b, 0, i))
    vec = pl.BlockSpec((1, 1, D), lambda b, i: (b, 0, 0))
    full = lambda a: pl.BlockSpec(a.shape, lambda b, i: (0,) * a.ndim)
    wo = p["w_out"].astype(MXU_DTYPE)
    wom, woc, wod = wo[:ATT_W], wo[ATT_W:ATT_W + CONV_CH], wo[ATT_W + CONV_CH:]
    pad_cols = lambda w: jnp.zeros((w.shape[0], LANES), F32).at[:, :w.shape[1]].set(w)
    pad_row = lambda v: jnp.zeros((1, LANES), F32).at[0, :v.shape[0]].set(v)
    small = [p["conv_w"], p["conv_b"].reshape(1, -1), p["conv_ln_g"].reshape(1, -1),
             p["conv_ln_b"].reshape(1, -1), p["moba_norm_g"].reshape(1, -1), p["dsa_norm_g"].reshape(1, -1),
             wom, woc, wod]
    tail = [p["norm2_g"].reshape(1, D)]
    tril = (jnp.arange(tm)[:, None] > jnp.arange(tm)[None, :]).astype(MXU_DTYPE)
    rt = [pad_cols(p["router_group_w"]), pad_row(p["router_group_b"]),
          pad_cols(p["router_expert_w"]), pad_row(p["router_expert_b"]), tril]
    sds = jax.ShapeDtypeStruct
    return pl.pallas_call(
        functools.partial(_mix_kernel, tm=tm, halo=halo),
        grid=(B, S // tm),
        in_specs=[row(D), colT, colT, row(CONV_CH),
                  pl.BlockSpec((1, halo, CONV_CH), lambda b, i: (b, jnp.maximum(i * (tm // halo) - 1, 0), 0))]
                 + [full(a) for a in small] + [vec] + [full(a) for a in tail] + [vec, vec] + [full(a) for a in rt],
        out_specs=[row(D), row(D), pl.BlockSpec((8, tm), lambda b, i: (0, b * (S // tm) + i)), row(8),
                   pl.BlockSpec((1, LANES), lambda b, i: (0, 0))],
        out_shape=[sds((B, S, D), F32), sds((B, S, D), F32), sds((8, B * S), F32), sds((B, S, 8), F32),
                   sds((1, LANES), F32)],
        scratch_shapes=[pltpu.VMEM((halo + 2 * tm, CONV_CH), F32), pltpu.VMEM((1, LANES), F32)],
        compiler_params=_params(("arbitrary", "arbitrary")),
        name="mix_outproj_route",
    )(x, ymT, ydT, u, u, *small, g1, *tail, sc2, sh2, *rt)


ROW_TILE = 512
DMA_UNROLL = 8


def _dispatch_kernel(dest_ref, h_ref, xs_init_ref, xs_ref, sem, *, tm, T):
    del xs_init_ref
    base = pl.program_id(0) * tm

    def issue(t, _):
        for k in range(2):
            d = dest_ref[k * T + base + t]
            pltpu.make_async_copy(h_ref.at[pl.ds(t, 1)], xs_ref.at[pl.ds(d, 1)], sem).start()
        return 0
    lax.fori_loop(0, tm, issue, 0, unroll=DMA_UNROLL)
    for k in range(2):
        pltpu.make_async_copy(h_ref, xs_ref.at[pl.ds(0, tm)], sem).wait()


def _ffn_kernel(blk_e_ref, nvalid_ref, x_ref, w1_ref, w3_ref, w2_ref, y_ref, w1b, w3b, w2b):
    i = pl.program_id(0)

    @pl.when(jnp.logical_or(i == 0, blk_e_ref[i] != blk_e_ref[jnp.maximum(i - 1, 0)]))
    def _():
        w1b[...] = w1_ref[0].astype(MXU_DTYPE)
        w3b[...] = w3_ref[0].astype(MXU_DTYPE)
        w2b[...] = w2_ref[0].astype(MXU_DTYPE)

    @pl.when(nvalid_ref[i] > 0)
    def _():
        xb = x_ref[...].astype(MXU_DTYPE)
        a = _dot(xb, w1b[...])
        hid = (a * jax.nn.sigmoid(a)) * _dot(xb, w3b[...])
        y_ref[...] = _dot(hid.astype(MXU_DTYPE), w2b[...])

    @pl.when(nvalid_ref[i] == 0)
    def _():
        y_ref[...] = jnp.zeros_like(y_ref)


def _combine_kernel(dest_ref, y_hbm, x_ref, gate_ref, g2_ref, fg_ref, o_ref, ybuf, sem, *, tm, T, final):
    base = pl.program_id(0) * tm

    def issue(t, _):
        for k in range(2):
            d = dest_ref[k * T + base + t]
            pltpu.make_async_copy(y_hbm.at[pl.ds(d, 1)], ybuf.at[k, pl.ds(t, 1)], sem).start()
        return 0
    lax.fori_loop(0, tm, issue, 0, unroll=DMA_UNROLL)
    for k in range(2):
        pltpu.make_async_copy(y_hbm.at[pl.ds(0, tm)], ybuf.at[k], sem).wait()
    g = gate_ref[...]
    x = x_ref[...] + g2_ref[0] * (g[:, 0:1] * ybuf[0] + g[:, 1:2] * ybuf[1])
    if final:
        x = _rms(x) * fg_ref[...]
    o_ref[...] = x


def _moe_ffn(h2, slot, counts, w1, w3, w2):
    T, D = h2.shape
    P = 2 * T + N_EXPERTS * MOE_BLOCK
    nblk = P // MOE_BLOCK
    tm = min(ROW_TILE, T)
    counts = counts[0, :N_EXPERTS].astype(I32)
    padded = (counts + MOE_BLOCK - 1) // MOE_BLOCK * MOE_BLOCK
    pad_end = jnp.cumsum(padded)
    pad_start = pad_end - padded
    experts = jnp.arange(N_EXPERTS, dtype=I32)
    e, rank = slot[0:2].astype(I32), slot[2:4].astype(I32)
    seg_start = jnp.sum(jnp.where(e[..., None] == experts, pad_start, 0), axis=-1)
    dest = (seg_start + rank).reshape(-1)
    blk_start = jnp.arange(nblk, dtype=I32) * MOE_BLOCK
    blk_e = jnp.minimum(jnp.sum((pad_end[None, :] <= blk_start[:, None]).astype(I32), axis=1), N_EXPERTS - 1)
    seg_end = jnp.sum(jnp.where(blk_e[:, None] == experts, pad_start + counts, 0), axis=-1)
    nvalid = jnp.clip(seg_end - blk_start, 0, MOE_BLOCK).astype(I32)

    xs = pl.pallas_call(
        functools.partial(_dispatch_kernel, tm=tm, T=T),
        grid_spec=pltpu.PrefetchScalarGridSpec(
            num_scalar_prefetch=1, grid=(T // tm,),
            in_specs=[pl.BlockSpec((tm, D), lambda i, d: (i, 0)), pl.BlockSpec(memory_space=pl.ANY)],
            out_specs=pl.BlockSpec(memory_space=pl.ANY),
            scratch_shapes=[pltpu.SemaphoreType.DMA]),
        out_shape=jax.ShapeDtypeStruct((P, D), F32),
        input_output_aliases={2: 0},
        compiler_params=_params(("arbitrary",)),
        name="moe_dispatch",
    )(dest, h2, jnp.zeros((P, D), F32))

    ys = pl.pallas_call(
        _ffn_kernel,
        grid_spec=pltpu.PrefetchScalarGridSpec(
            num_scalar_prefetch=2, grid=(nblk,),
            in_specs=[pl.BlockSpec((MOE_BLOCK, D), lambda i, be, nv: (i, 0)),
                      pl.BlockSpec((1, D, D_EXPERT), lambda i, be, nv: (be[i], 0, 0)),
                      pl.BlockSpec((1, D, D_EXPERT), lambda i, be, nv: (be[i], 0, 0)),
                      pl.BlockSpec((1, D_EXPERT, D), lambda i, be, nv: (be[i], 0, 0))],
            out_specs=pl.BlockSpec((MOE_BLOCK, D), lambda i, be, nv: (i, 0)),
            scratch_shapes=[pltpu.VMEM((D, D_EXPERT), MXU_DTYPE), pltpu.VMEM((D, D_EXPERT), MXU_DTYPE),
                            pltpu.VMEM((D_EXPERT, D), MXU_DTYPE)]),
        out_shape=jax.ShapeDtypeStruct((P, D), F32),
        compiler_params=_params(("arbitrary",)),
        name="moe_ffn",
    )(blk_e, nvalid, xs, w1, w3, w2)
    return ys, dest


def _moe_combine(x1, ys, dest, gates, g2, final_g):
    B, S, D = x1.shape
    T = B * S
    tm = min(ROW_TILE, S)
    final = final_g is not None
    fg = (final_g if final else jnp.ones((D,), F32)).reshape(1, D)
    out = pl.pallas_call(
        functools.partial(_combine_kernel, tm=tm, T=T, final=final),
        grid_spec=pltpu.PrefetchScalarGridSpec(
            num_scalar_prefetch=1, grid=(T // tm,),
            in_specs=[pl.BlockSpec(memory_space=pl.ANY),
                      pl.BlockSpec((tm, D), lambda i, d: (i, 0)),
                      pl.BlockSpec((tm, 8), lambda i, d: (i, 0)),
                      pl.BlockSpec((1, 1, D), lambda i, d: (i // (S // tm), 0, 0)),
                      pl.BlockSpec((1, D), lambda i, d: (0, 0))],
            out_specs=pl.BlockSpec((tm, D), lambda i, d: (i, 0)),
            scratch_shapes=[pltpu.VMEM((2, tm, D), F32), pltpu.SemaphoreType.DMA]),
        out_shape=jax.ShapeDtypeStruct((T, D), F32),
        compiler_params=_params(("arbitrary",)),
        name="moe_combine",
    )(dest, ys, x1.reshape(T, D), gates.reshape(T, 8), g2, fg)
    return out.reshape(B, S, D)


def kernel(x, c, ada_w, ada_b, norm1_g, w_in, conv_w, conv_b, conv_ln_g, conv_ln_b, moba_norm_g, dsa_norm_g, w_out, norm2_g, router_group_w, router_group_b, router_expert_w, router_expert_b, expert_w1, expert_w3, expert_w2, final_g):
    B, S, D = x.shape
    L = ada_w.shape[0]
    mod = _modulation(c, ada_w, ada_b)
    for l in range(L):
        sh1, sc1, g1, sh2, sc2, g2 = [m[:, None, :] for m in jnp.split(mod[l], 6, axis=-1)]
        qmT, km, vmT, kmean, u, qdT, kd, vdT, qiT, ki, wT = _inproj(
            x, norm1_g[l], sc1, sh1, _split_w_in(w_in[l]))
        y_moba = _moba(qmT, km, vmT, kmean)
        y_dsa = _dsa(qdT, kd, vdT, qiT, ki, wT)
        p = dict(conv_w=conv_w[l], conv_b=conv_b[l], conv_ln_g=conv_ln_g[l], conv_ln_b=conv_ln_b[l],
                 moba_norm_g=moba_norm_g[l], dsa_norm_g=dsa_norm_g[l], w_out=w_out[l], norm2_g=norm2_g[l],
                 router_group_w=router_group_w[l], router_group_b=router_group_b[l],
                 router_expert_w=router_expert_w[l], router_expert_b=router_expert_b[l])
        x1, h2, slot, gates, counts = _mix(x, y_moba, y_dsa, u, p, g1, sc2, sh2)
        ys, dest = _moe_ffn(h2.reshape(B * S, D), slot, counts, expert_w1[l], expert_w3[l], expert_w2[l])
        x = _moe_combine(x1, ys, dest, gates, g2, final_g if l == L - 1 else None)
    return x
```

```python
import functools

import jax
import jax.numpy as jnp
from jax import lax
from jax.experimental import pallas as pl
from jax.experimental.pallas import tpu as pltpu

F32 = jnp.float32
I32 = jnp.int32
I16 = jnp.int16
MXU_DTYPE = jnp.bfloat16

HEAD_DIM = 64
MOBA_HEADS = 6
MOBA_BLOCK = 256
MOBA_TOPK = 3
CONV_CH = 256
CONV_WIDTH = 31
DSA_HEADS = 6
DSA_TOPK = 256
IDX_HEADS = 8
IDX_DIM = 64
ATT_W = 384
N_GROUPS = 4
EXPERTS_PER_GROUP = 8
N_EXPERTS = 32
D_EXPERT = 512
MOE_BLOCK = 256
EPS = 1e-6

LANES = 128
PACK_ROWS = 16
PAIR_W = 2 * HEAD_DIM
V_ROWS = HEAD_DIM + PACK_ROWS
VT_W = 6 * V_ROWS
NEG = -1e30
INT_MIN = -2 ** 31
VMEM_LIMIT = 56 * 1024 * 1024
LOG2_E = 1.4426950408889634
Q_SCALE = HEAD_DIM ** -0.5 * LOG2_E
IDX_SCALE = IDX_HEADS ** -0.5 * IDX_DIM ** -0.5


def _params(sem):
    return pltpu.CompilerParams(dimension_semantics=sem, vmem_limit_bytes=VMEM_LIMIT)


def _rms(xf):
    return xf * lax.rsqrt(jnp.mean(xf * xf, axis=-1, keepdims=True) + EPS)


def _dot(a, b):
    return jnp.dot(a, b, preferred_element_type=F32)


def _dot_nt(a, b):
    return lax.dot_general(a, b, (((1,), (1,)), ((), ())), preferred_element_type=F32)


def _resident(block_shape, index_map):
    return pl.BlockSpec(block_shape, index_map, pipeline_mode=pl.Buffered(1))


def _mod_kernel(c_ref, w_ref, b_ref, o_ref):
    c = c_ref[...]
    o_ref[0] = _dot(c * jax.nn.sigmoid(c), w_ref[0]) + b_ref[0]


def _modulation(c, ada_w, ada_b):
    L, D, N = ada_w.shape
    B = c.shape[0]
    rows = 8
    cp = jnp.zeros((rows, D), F32).at[:B].set(c)
    tn = 1536
    out = pl.pallas_call(
        _mod_kernel,
        grid=(L, N // tn),
        in_specs=[pl.BlockSpec((rows, D), lambda l, j: (0, 0)),
                  pl.BlockSpec((1, D, tn), lambda l, j: (l, 0, j)),
                  pl.BlockSpec((1, 1, tn), lambda l, j: (l, 0, j))],
        out_specs=pl.BlockSpec((1, rows, tn), lambda l, j: (l, 0, j)),
        out_shape=jax.ShapeDtypeStruct((L, rows, N), F32),
        compiler_params=_params(("arbitrary", "arbitrary")),
        name="adaln_mod",
    )(cp, ada_w, ada_b.reshape(L, 1, N))
    return out[:, :B]


def _store_values_with_ones(vT_ref, vT, tm):
    ones = jnp.ones((PACK_ROWS, tm), vT_ref.dtype)
    for h in range(ATT_W // HEAD_DIM):
        vT_ref[0, h * V_ROWS:h * V_ROWS + HEAD_DIM, :] = vT[h * HEAD_DIM:(h + 1) * HEAD_DIM].astype(vT_ref.dtype)
        vT_ref[0, h * V_ROWS + HEAD_DIM:(h + 1) * V_ROWS, :] = ones


def _inproj_kernel(x_ref, g_ref, sc_ref, sh_ref, wqmT, wkm, wvmT, wglu, wqdT, wkd, wvdT, wqiT, wki, wwiT,
                   qmT_ref, km_ref, vmT_ref, kmean_ref, u_ref, qdT_ref, kd_ref, vdT_ref, qiT_ref, ki_ref, wiT_ref,
                   *, tm):
    h = _rms(x_ref[0]) * g_ref[...] * (1.0 + sc_ref[0]) + sh_ref[0]
    hb = h.astype(MXU_DTYPE)

    qmT_ref[0] = _dot_nt(wqmT[...], hb).astype(qmT_ref.dtype)
    k_rows = _dot(hb, wkm[...])
    km_ref[0] = k_rows.astype(km_ref.dtype)
    for c in range(tm // MOBA_BLOCK):
        kmean_ref[0, c] = jnp.mean(k_rows[c * MOBA_BLOCK:(c + 1) * MOBA_BLOCK], axis=0, keepdims=True)
    _store_values_with_ones(vmT_ref, _dot_nt(wvmT[...], hb), tm)
    glu = _dot(hb, wglu[...])
    u_ref[0] = glu[:, :CONV_CH] * jax.nn.sigmoid(glu[:, CONV_CH:])
    qdT_ref[0] = _dot_nt(wqdT[...], hb).astype(qdT_ref.dtype)
    kd_ref[0] = _dot(hb, wkd[...]).astype(kd_ref.dtype)
    _store_values_with_ones(vdT_ref, _dot_nt(wvdT[...], hb), tm)
    qiT_ref[0] = _dot_nt(wqiT[...], hb).astype(qiT_ref.dtype)
    ki_ref[0] = _dot(hb, wki[...]).astype(ki_ref.dtype)
    wiT_ref[0] = _dot_nt(wwiT[...], hb) * IDX_SCALE


_W_NAMES = ["wqmT", "wkm", "wvmT", "wglu", "wqdT", "wkd", "wvdT", "wqiT", "wki", "wwiT"]


def _split_w_in(w_in):
    offs = [0, 384, 768, 1152, 1408, 1664, 2048, 2432, 2816, 3328, 3392, 3400]
    piece = lambda k: w_in[:, offs[k]:offs[k + 1]]
    c = lambda w: w.astype(MXU_DTYPE)
    return dict(
        wqmT=c(piece(0).T * Q_SCALE), wkm=c(piece(1)), wvmT=c(piece(2).T),
        wglu=c(w_in[:, offs[3]:offs[5]]),
        wqdT=c(piece(5).T * Q_SCALE), wkd=c(piece(6)), wvdT=c(piece(7).T),
        wqiT=c(piece(8).T), wki=c(piece(9)), wwiT=c(piece(10).T))


def _inproj(x, n1g, sc1, sh1, w):
    B, S, D = x.shape
    tm = 512
    nb_t = tm // MOBA_BLOCK
    row = lambda width: pl.BlockSpec((1, tm, width), lambda b, i: (b, i, 0))
    col = lambda height: pl.BlockSpec((1, height, tm), lambda b, i: (b, 0, i))
    vec = pl.BlockSpec((1, 1, D), lambda b, i: (b, 0, 0))
    full = lambda a: pl.BlockSpec(a.shape, lambda b, i: (0,) * a.ndim)

    ins = [x, n1g.reshape(1, D), sc1, sh1] + [w[k] for k in _W_NAMES]
    in_specs = [row(D), pl.BlockSpec((1, D), lambda b, i: (0, 0)), vec, vec] + [full(w[k]) for k in _W_NAMES]

    sds = jax.ShapeDtypeStruct
    QI_W = IDX_HEADS * IDX_DIM
    out_shape = [
        sds((B, ATT_W, S), MXU_DTYPE), sds((B, S, ATT_W), MXU_DTYPE), sds((B, VT_W, S), MXU_DTYPE),
        sds((B, S // MOBA_BLOCK, 1, ATT_W), F32),
        sds((B, S, CONV_CH), F32),
        sds((B, ATT_W, S), MXU_DTYPE), sds((B, S, ATT_W), MXU_DTYPE), sds((B, VT_W, S), MXU_DTYPE),
        sds((B, QI_W, S), MXU_DTYPE), sds((B, S, IDX_DIM), MXU_DTYPE), sds((B, IDX_HEADS, S), F32)]
    out_specs = [
        col(ATT_W), row(ATT_W), col(VT_W),
        pl.BlockSpec((1, nb_t, 1, ATT_W), lambda b, i: (b, i, 0, 0)),
        row(CONV_CH),
        col(ATT_W), row(ATT_W), col(VT_W),
        col(QI_W), row(IDX_DIM), col(IDX_HEADS)]

    return pl.pallas_call(
        functools.partial(_inproj_kernel, tm=tm),
        grid=(B, S // tm),
        in_specs=in_specs, out_specs=out_specs, out_shape=out_shape,
        compiler_params=_params(("arbitrary", "arbitrary")),
        name="inproj",
    )(*ins)


def _pair_queries(qT_ref, p, TQ):
    first = lax.broadcasted_iota(I32, (PAIR_W, TQ), 0) < HEAD_DIM
    qpair = qT_ref[0, p * PAIR_W:(p + 1) * PAIR_W, :]
    zero = jnp.zeros_like(qpair)
    return jnp.where(first, qpair, zero), jnp.where(first, zero, qpair)


def _fold8(x, op):
    return op(x.reshape(x.shape[0] // 8, 8, x.shape[1]), axis=0)


def _init_weights_scratch(p_ref):
    @pl.when(jnp.logical_and(pl.program_id(0) == 0, pl.program_id(1) == 0))
    def _():
        p_ref[...] = jnp.zeros_like(p_ref)


def _pair_attention(k_ref, vT_ref, p, qpads, n_tiles, bias_fn, s_ref, p_ref, *, TK, RC, TQ):
    n_chunks = TK // RC
    ps = slice(p * PAIR_W, (p + 1) * PAIR_W)
    vrows = [slice((2 * p + r) * V_ROWS, (2 * p + r + 1) * V_ROWS) for r in range(2)]

    def scores(ks, c, r):
        rows = pl.ds(ks + c * RC, RC)
        return _dot(k_ref[0, rows, ps], qpads[r]) + bias_fn(r, ks, c)

    def pv(r, ks, alpha, acc, live=None):
        w = p_ref[r]
        if live is not None:
            w = jnp.where(live, w, jnp.zeros_like(w))
        return alpha * acc + _dot(vT_ref[0, vrows[r], pl.ds(pl.multiple_of(ks, TK), TK)], w)

    def section(ks_next, r_next, r_cur, m_new, deferred):
        runmax = jnp.full((8, TQ), NEG, F32)
        out = None
        for c in range(n_chunks):
            cs = slice(c * RC, (c + 1) * RC)
            s = scores(ks_next, c, r_next)
            s_ref[r_next, cs, :] = s
            runmax = jnp.maximum(runmax, _fold8(s, jnp.max))
            p_ref[r_cur, cs, :] = jnp.exp2(s_ref[r_cur, cs, :] - m_new).astype(p_ref.dtype)
            if c == 0:
                out = deferred()
        return jnp.max(runmax, axis=0, keepdims=True), out

    runmax = jnp.full((8, TQ), NEG, F32)
    for c in range(n_chunks):
        s = scores(0, c, 0)
        s_ref[0, c * RC:(c + 1) * RC, :] = s
        runmax = jnp.maximum(runmax, _fold8(s, jnp.max))
    tmax0 = jnp.max(runmax, axis=0, keepdims=True)

    def tile(j, carry):
        m0, a0, m1, a1, tmax, alpha1, ks1 = carry
        ks = pl.multiple_of(j * TK, TK)
        ks_next = pl.multiple_of(jnp.minimum(j + 1, n_tiles - 1) * TK, TK)
        m0n = jnp.maximum(m0, tmax)
        tmax1, a1 = section(ks, 1, 0, m0n, lambda: pv(1, ks1, alpha1, a1, live=j > 0))
        alpha0 = jnp.exp2(m0 - m0n)
        m1n = jnp.maximum(m1, tmax1)
        tmax0n, a0 = section(ks_next, 0, 1, m1n, lambda: pv(0, ks, alpha0, a0))
        return m0n, a0, m1n, a1, tmax0n, jnp.exp2(m1 - m1n), ks

    m_init, a_init = jnp.full((1, TQ), NEG, F32), jnp.zeros((V_ROWS, TQ), F32)
    _, a0, _, a1, _, alpha1, ks1 = lax.fori_loop(
        0, n_tiles, tile, (m_init, a_init, m_init, a_init, tmax0, jnp.ones((1, TQ), F32), jnp.int32(0)))
    a1 = pv(1, ks1, alpha1, a1)
    return [a[0:HEAD_DIM] / a[HEAD_DIM:HEAD_DIM + 1] for a in (a0, a1)]


def _moba_kernel(qT_ref, k_ref, vT_ref, km_ref, o_ref, sb_ref, cb_ref, s_ref, p_ref, *, nb, TK, RC):
    TQ = MOBA_BLOCK
    i = pl.program_id(1)
    _init_weights_scratch(p_ref)
    blk = lax.broadcasted_iota(I32, (nb, TQ), 0)
    past = blk < i
    cb_ref[...] = jnp.where(lax.broadcasted_iota(I32, (TQ, TQ), 0) <= lax.broadcasted_iota(I32, (TQ, TQ), 1),
                            0.0, NEG)
    blocks_per_tile = TK // TQ
    n_tiles = (i + blocks_per_tile) // blocks_per_tile

    for p in range(MOBA_HEADS // 2):
        ps = slice(p * PAIR_W, (p + 1) * PAIR_W)
        hrows = [slice((2 * p + r) * HEAD_DIM, (2 * p + r + 1) * HEAD_DIM) for r in range(2)]
        qpads = _pair_queries(qT_ref, p, TQ)
        for r in range(2):
            g = jnp.where(past, _dot(km_ref[0, :, ps], qpads[r]), -jnp.inf)
            sel = jnp.zeros((nb, TQ), I32)
            for _ in range(MOBA_TOPK):
                mx = jnp.max(g, axis=0, keepdims=True)
                first = jnp.min(jnp.where(g == mx, blk, nb), axis=0, keepdims=True)
                hit = blk == first
                sel = jnp.where(hit, 1, sel)
                g = jnp.where(hit, -jnp.inf, g)
            sb_ref[r] = jnp.where(past, jnp.where(sel > 0, 0.0, NEG), NEG)

        def bias_fn(r, ks, c):
            b = ks // TQ + (c * RC) // TQ
            within = (c * RC) % TQ
            row = jnp.broadcast_to(sb_ref[r, pl.ds(b, 1), :], (RC, TQ))
            return jnp.where(b == i, cb_ref[within:within + RC, :], row)

        outs = _pair_attention(k_ref, vT_ref, p, qpads, n_tiles, bias_fn, s_ref, p_ref,
                               TK=TK, RC=RC, TQ=TQ)
        for r in range(2):
            o_ref[0, hrows[r], :] = outs[r]


def _moba(qT, k, vT, kmean):
    B, W, S = qT.shape
    nb = S // MOBA_BLOCK
    TK, RC = min(1024, S), 128
    assert S % TK == 0 and TK % MOBA_BLOCK == 0 and MOBA_BLOCK % RC == 0
    km = kmean.reshape(B, nb, W).astype(MXU_DTYPE)
    return pl.pallas_call(
        functools.partial(_moba_kernel, nb=nb, TK=TK, RC=RC),
        grid=(B, nb),
        in_specs=[pl.BlockSpec((1, W, MOBA_BLOCK), lambda b, i: (b, 0, i)),
                  _resident((1, S, W), lambda b, i: (b, 0, 0)),
                  _resident((1, vT.shape[1], S), lambda b, i: (b, 0, 0)),
                  _resident((1, nb, W), lambda b, i: (b, 0, 0))],
        out_specs=pl.BlockSpec((1, W, MOBA_BLOCK), lambda b, i: (b, 0, i)),
        out_shape=jax.ShapeDtypeStruct((B, W, S), F32),
        scratch_shapes=[pltpu.VMEM((2, nb, MOBA_BLOCK), F32), pltpu.VMEM((MOBA_BLOCK, MOBA_BLOCK), F32),
                        pltpu.VMEM((2, TK, MOBA_BLOCK), F32), pltpu.VMEM((2, TK, MOBA_BLOCK), MXU_DTYPE)],
        compiler_params=_params(("arbitrary", "arbitrary")),
        name="moba",
    )(qT, k, vT, km)


def _order_key(x):
    b = pltpu.bitcast(x, I32)
    return b ^ ((b >> 31) & 0x7FFFFFFF)


def _dsa_kernel(qT_ref, k_ref, vT_ref, qiT_ref, ki_ref, w_ref, o_ref, hi_ref, lo_ref, bias_ref, s_ref, p_ref,
                *, TQ, TK, CH, TKA, RC, topk):
    i = pl.program_id(1)
    _init_weights_scratch(p_ref)
    n_t = ((i + 1) * TQ + TK - 1) // TK
    krow = lax.broadcasted_iota(I32, (TK, TQ), 0)
    qcol = i * TQ + lax.broadcasted_iota(I32, (TK, TQ), 1)

    def index_tile(j, _):
        ks = pl.ds(pl.multiple_of(j * TK, TK), TK)
        ki = ki_ref[0, ks, :]
        acc = jnp.zeros((TK, TQ), F32)
        for h in range(IDX_HEADS):
            s = _dot(ki, qiT_ref[0, h * IDX_DIM:(h + 1) * IDX_DIM, :])
            acc = acc + w_ref[0, h:h + 1, :] * jnp.maximum(s, 0.0)
        key = jnp.where(j * TK + krow <= qcol, _order_key(acc), INT_MIN)
        hi_ref[ks, :] = (key >> 16).astype(I16)
        lo_ref[ks, :] = ((key & 0xFFFF) - 32768).astype(I16)
        return 0
    lax.fori_loop(0, n_t, index_tile, 0)

    n_ch = n_t * (TK // CH)
    one, zero = jnp.ones((PACK_ROWS, TQ), I16), jnp.zeros((PACK_ROWS, TQ), I16)
    rows16 = lax.broadcasted_iota(I32, (PACK_ROWS, TQ), 0)

    def rows_to_i16(v):
        return jnp.broadcast_to(v, (PACK_ROWS, TQ)).astype(I16)

    n_sub = CH // PACK_ROWS

    def chunk_blocks(refs, c):
        r0 = pl.multiple_of(c * CH, CH)
        vals = [ref[pl.ds(r0, CH), :] for ref in refs]
        return [([v[u * PACK_ROWS:(u + 1) * PACK_ROWS] for v in vals], r0 + u * PACK_ROWS) for u in range(n_sub)]

    def count(refs, flag):
        def chunk(c, parts):
            parts = list(parts)
            for u, (blocks, r) in enumerate(chunk_blocks(refs, c)):
                parts[u % len(parts)] = parts[u % len(parts)] + flag(*blocks, r)
            return tuple(parts)
        parts = lax.fori_loop(0, n_ch, chunk, (zero,) * 4)
        part = (parts[0] + parts[1]) + (parts[2] + parts[3])
        return jnp.sum(part.astype(I32), axis=0, keepdims=True)

    def search16(ref, k_needed):
        def step(t, ans_u):
            cand_u = ans_u | lax.shift_left(jnp.int32(1), 15 - t)
            c16 = rows_to_i16(cand_u - 32768)
            cnt = count([ref], lambda x, r: jnp.where(x >= c16, one, zero))
            return jnp.where(cnt >= k_needed, cand_u, ans_u)
        return lax.fori_loop(0, 16, step, jnp.zeros((1, TQ), I32)) - 32768

    P = search16(hi_ref, topk)
    P = jnp.maximum(P, -32767)
    P16 = rows_to_i16(P)
    n_hi_gt = count([hi_ref], lambda hi, r: jnp.where(hi > P16, one, zero))

    lowest = jnp.full((PACK_ROWS, TQ), -32768, I16)

    def restrict_lo(c, _):
        out = [jnp.where(hi == P16, lo, lowest) for (hi, lo), _ in chunk_blocks([hi_ref, lo_ref], c)]
        lo_ref[pl.ds(pl.multiple_of(c * CH, CH), CH), :] = jnp.concatenate(out, axis=0)
        return 0
    lax.fori_loop(0, n_ch, restrict_lo, 0)
    Q = search16(lo_ref, topk - n_hi_gt)
    Q16 = rows_to_i16(Q)

    def in_bucket_eq(hi, lo):
        return jnp.where(hi == P16, jnp.where(lo == Q16, one, zero), zero)
    n_gt = n_hi_gt + count([lo_ref], lambda lo, r: jnp.where(lo > Q16, one, zero))
    n_eq = count([hi_ref, lo_ref], lambda hi, lo, r: in_bucket_eq(hi, lo))
    need = topk - n_gt
    tied = n_eq > need
    any_tied = jnp.max(jnp.where(tied, 1, 0)) > 0

    def last_tie_row(_):
        def step(t, ans):
            cand = ans | lax.shift_left(jnp.int32(1), 14 - t)
            c16 = rows_to_i16(cand)
            cnt = count([hi_ref, lo_ref],
                        lambda hi, lo, r: jnp.where((r + rows16).astype(I16) < c16, in_bucket_eq(hi, lo), zero))
            return jnp.where(cnt < need, cand, ans)
        return lax.fori_loop(0, 15, step, jnp.zeros((1, TQ), I32))
    jmax = lax.cond(any_tied, last_tie_row, lambda _: jnp.full((1, TQ), 32767, I32), 0)
    J16 = rows_to_i16(jnp.where(tied, jmax, 32767))

    def to_bias(c, _):
        out = []
        for (hi, lo), r in chunk_blocks([hi_ref, lo_ref], c):
            tie_keep = jnp.where((r + rows16).astype(I16) <= J16, one, zero)
            in_p = jnp.where(lo > Q16, one, jnp.where(lo == Q16, tie_keep, zero))
            keep = jnp.where(hi > P16, one, jnp.where(hi == P16, in_p, zero))
            out.append(jnp.where(keep.astype(I32) > 0, 0.0, NEG))
        bias_ref[pl.ds(pl.multiple_of(c * CH, CH), CH), :] = jnp.concatenate(out, axis=0)
        return 0
    lax.fori_loop(0, n_ch, to_bias, 0)

    tiles_per_att = TKA // TK
    n_att = (n_t + tiles_per_att - 1) // tiles_per_att
    for extra in range(1, tiles_per_att):
        @pl.when(n_t % tiles_per_att == extra)
        def _():
            for t in range(tiles_per_att - extra):
                bias_ref[pl.ds(pl.multiple_of((n_t + t) * TK, TK), TK), :] = jnp.full((TK, TQ), NEG, F32)

    for p in range(DSA_HEADS // 2):
        hrows = [slice((2 * p + r) * HEAD_DIM, (2 * p + r + 1) * HEAD_DIM) for r in range(2)]
        qpads = _pair_queries(qT_ref, p, TQ)
        outs = _pair_attention(k_ref, vT_ref, p, qpads, n_att,
                               lambda r, ks, c: bias_ref[pl.ds(ks + c * RC, RC), :], s_ref, p_ref,
                               TK=TKA, RC=RC, TQ=TQ)
        for r in range(2):
            o_ref[0, hrows[r], :] = outs[r]


def _dsa(qT, k, vT, qiT, ki, wT):
    B, W, S = qT.shape
    assert S < 2 ** 15, "key indices are compared as int16"
    topk = min(DSA_TOPK, S // 4)
    TQ, TK, CH = 256, min(512, S), 256
    TKA, RC = min(1024, S), 128
    assert S % TKA == 0 and TKA % TK == 0
    return pl.pallas_call(
        functools.partial(_dsa_kernel, TQ=TQ, TK=TK, CH=CH, TKA=TKA, RC=RC, topk=topk),
        grid=(B, S // TQ),
        in_specs=[pl.BlockSpec((1, W, TQ), lambda b, i: (b, 0, i)),
                  _resident((1, S, W), lambda b, i: (b, 0, 0)),
                  _resident((1, vT.shape[1], S), lambda b, i: (b, 0, 0)),
                  pl.BlockSpec((1, IDX_HEADS * IDX_DIM, TQ), lambda b, i: (b, 0, i)),
                  _resident((1, S, IDX_DIM), lambda b, i: (b, 0, 0)),
                  pl.BlockSpec((1, IDX_HEADS, TQ), lambda b, i: (b, 0, i))],
        out_specs=pl.BlockSpec((1, W, TQ), lambda b, i: (b, 0, i)),
        out_shape=jax.ShapeDtypeStruct((B, W, S), F32),
        scratch_shapes=[pltpu.VMEM((S, TQ), I16), pltpu.VMEM((S, TQ), I16), pltpu.VMEM((S, TQ), F32),
                        pltpu.VMEM((2, TKA, TQ), F32), pltpu.VMEM((2, TKA, TQ), MXU_DTYPE)],
        compiler_params=_params(("arbitrary", "arbitrary")),
        name="dsa",
    )(qT, k, vT, qiT, ki, wT)


def _mix_kernel(x_ref, ym_ref, yd_ref, u_ref, up_ref, cw_ref, cb_ref, lg_ref, lb_ref, mg_ref, dg_ref,
                wom_ref, woc_ref, wod_ref, g1_ref, n2_ref, sc_ref, sh_ref, rgw_ref, rgb_ref, rew_ref, reb_ref,
                tril_ref, x1_ref, h2_ref, slot_ref, gate_ref, counts_ref, win_ref, cnt_ref, *, tm, halo):
    i = pl.program_id(1)
    win_ref[0:halo, :] = jnp.where(i > 0, up_ref[0], 0.0)
    win_ref[halo:halo + tm, :] = u_ref[0]
    rc = 128
    off = halo - (CONV_WIDTH - 1)
    cw = cw_ref[...]
    for r0 in range(0, tm, rc):
        acc = jnp.zeros((rc, CONV_CH), F32) + cb_ref[...]
        for k in range(CONV_WIDTH):
            acc = acc + cw[k:k + 1, :] * win_ref[r0 + off + k:r0 + off + k + rc, :]
        mu = jnp.mean(acc, axis=-1, keepdims=True)
        xc = acc - mu
        yn = xc * lax.rsqrt(jnp.mean(xc * xc, axis=-1, keepdims=True) + EPS) * lg_ref[...] + lb_ref[...]
        win_ref[halo + tm + r0:halo + tm + r0 + rc, :] = yn * jax.nn.sigmoid(yn)
    yc = win_ref[halo + tm:halo + 2 * tm, :]

    ymn = (_rms(ym_ref[0].T) * mg_ref[...]).astype(MXU_DTYPE)
    ydn = (_rms(yd_ref[0].T) * dg_ref[...]).astype(MXU_DTYPE)
    proj = _dot(ymn, wom_ref[...]) + _dot(yc.astype(MXU_DTYPE), woc_ref[...]) + _dot(ydn, wod_ref[...])
    x1 = x_ref[0] + g1_ref[0] * proj
    x1_ref[0] = x1
    h2 = _rms(x1) * n2_ref[...] * (1.0 + sc_ref[0]) + sh_ref[0]
    h2_ref[0] = h2

    lane = lax.broadcasted_iota(I32, (tm, LANES), 1)
    gl = _dot(h2, rgw_ref[...]) + rgb_ref[...]
    gl = jnp.where(lane < N_GROUPS, gl, -jnp.inf)
    gmax = jnp.max(gl, axis=-1, keepdims=True)
    g_idx = jnp.min(jnp.where(gl == gmax, lane, LANES), axis=-1, keepdims=True)
    g_w = 1.0 / jnp.sum(jnp.exp(gl - gmax), axis=-1, keepdims=True)
    el = _dot(h2, rew_ref[...]) + reb_ref[...]
    in_group = jnp.logical_and(lane >= g_idx * EXPERTS_PER_GROUP, lane < (g_idx + 1) * EXPERTS_PER_GROUP)
    el = jnp.where(in_group, el, -jnp.inf)
    t1 = jnp.max(el, axis=-1, keepdims=True)
    e1 = jnp.min(jnp.where(el == t1, lane, LANES), axis=-1, keepdims=True)
    el2 = jnp.where(lane == e1, -jnp.inf, el)
    t2 = jnp.max(el2, axis=-1, keepdims=True)
    e2 = jnp.min(jnp.where(el2 == t2, lane, LANES), axis=-1, keepdims=True)
    r = jnp.exp(t2 - t1)
    p1 = 1.0 / (1.0 + r)
    lane8 = lax.broadcasted_iota(I32, (tm, 8), 1)
    gate_ref[0] = jnp.where(lane8 == 0, g_w * p1, jnp.where(lane8 == 1, g_w * p1 * r, 0.0))

    @pl.when(jnp.logical_and(pl.program_id(0) == 0, i == 0))
    def _():
        cnt_ref[...] = jnp.zeros_like(cnt_ref)
    cnt = cnt_ref[...]
    ranks = []
    for e_k in (e1, e2):
        onehot = jnp.where(lane == e_k, 1.0, 0.0)
        before = cnt + _dot(tril_ref[...], onehot.astype(MXU_DTYPE))
        ranks.append(jnp.sum(onehot * before, axis=-1, keepdims=True).astype(I32))
        cnt = cnt + jnp.sum(onehot, axis=0, keepdims=True)
    cnt_ref[...] = cnt
    counts_ref[...] = cnt
    cols = jnp.where(lane == 0, e1, jnp.where(lane == 1, e2, jnp.where(
        lane == 2, ranks[0], jnp.where(lane == 3, ranks[1], 0)))).astype(F32)
    slot_ref[...] = cols.T[0:8, :]


def _mix(x, ymT, ydT, u, p, g1, sc2, sh2):
    B, S, D = x.shape
    tm, halo = 512, 32
    row = lambda width: pl.BlockSpec((1, tm, width), lambda b, i: (b, i, 0))
    colT = pl.BlockSpec((1, ATT_W, tm), lambda b, i: (b, 0, i))
    vec = pl.BlockSpec((1, 1, D), lambda b, i: (b, 0, 0))
    full = lambda a: pl.BlockSpec(a.shape, lambda b, i: (0,) * a.ndim)
    wo = p["w_out"].astype(MXU_DTYPE)
    wom, woc, wod = wo[:ATT_W], wo[ATT_W:ATT_W + CONV_CH], wo[ATT_W + CONV_CH:]
    pad_cols = lambda w: jnp.zeros((w.shape[0], LANES), F32).at[:, :w.shape[1]].set(w)
    pad_row = lambda v: jnp.zeros((1, LANES), F32).at[0, :v.shape[0]].set(v)
    small = [p["conv_w"], p["conv_b"].reshape(1, -1), p["conv_ln_g"].reshape(1, -1),
             p["conv_ln_b"].reshape(1, -1), p["moba_norm_g"].reshape(1, -1), p["dsa_norm_g"].reshape(1, -1),
             wom, woc, wod]
    tail = [p["norm2_g"].reshape(1, D)]
    tril = (jnp.arange(tm)[:, None] > jnp.arange(tm)[None, :]).astype(MXU_DTYPE)
    rt = [pad_cols(p["router_group_w"]), pad_row(p["router_group_b"]),
          pad_cols(p["router_expert_w"]), pad_row(p["router_expert_b"]), tril]
    sds = jax.ShapeDtypeStruct
    return pl.pallas_call(
        functools.partial(_mix_kernel, tm=tm, halo=halo),
        grid=(B, S // tm),
        in_specs=[row(D), colT, colT, row(CONV_CH),
                  pl.BlockSpec((1, halo, CONV_CH), lambda b, i: (b, jnp.maximum(i * (tm // halo) - 1, 0), 0))]
                 + [full(a) for a in small] + [vec] + [full(a) for a in tail] + [vec, vec] + [full(a) for a in rt],
        out_specs=[row(D), row(D), pl.BlockSpec((8, tm), lambda b, i: (0, b * (S // tm) + i)), row(8),
                   pl.BlockSpec((1, LANES), lambda b, i: (0, 0))],
        out_shape=[sds((B, S, D), F32), sds((B, S, D), F32), sds((8, B * S), F32), sds((B, S, 8), F32),
                   sds((1, LANES), F32)],
        scratch_shapes=[pltpu.VMEM((halo + 2 * tm, CONV_CH), F32), pltpu.VMEM((1, LANES), F32)],
        compiler_params=_params(("arbitrary", "arbitrary")),
        name="mix_outproj_route",
    )(x, ymT, ydT, u, u, *small, g1, *tail, sc2, sh2, *rt)


ROW_TILE = 512
DMA_UNROLL = 8


def _dispatch_kernel(dest_ref, h_ref, xs_init_ref, xs_ref, sem, *, tm, T):
    del xs_init_ref
    base = pl.program_id(0) * tm

    def issue(t, _):
        for k in range(2):
            d = dest_ref[k * T + base + t]
            pltpu.make_async_copy(h_ref.at[pl.ds(t, 1)], xs_ref.at[pl.ds(d, 1)], sem).start()
        return 0
    lax.fori_loop(0, tm, issue, 0, unroll=DMA_UNROLL)
    for k in range(2):
        pltpu.make_async_copy(h_ref, xs_ref.at[pl.ds(0, tm)], sem).wait()


def _ffn_kernel(blk_e_ref, nvalid_ref, x_ref, w1_ref, w3_ref, w2_ref, y_ref, w1b, w3b, w2b):
    i = pl.program_id(0)

    @pl.when(jnp.logical_or(i == 0, blk_e_ref[i] != blk_e_ref[jnp.maximum(i - 1, 0)]))
    def _():
        w1b[...] = w1_ref[0].astype(MXU_DTYPE)
        w3b[...] = w3_ref[0].astype(MXU_DTYPE)
        w2b[...] = w2_ref[0].astype(MXU_DTYPE)

    @pl.when(nvalid_ref[i] > 0)
    def _():
        xb = x_ref[...].astype(MXU_DTYPE)
        a = _dot(xb, w1b[...])
        hid = (a * jax.nn.sigmoid(a)) * _dot(xb, w3b[...])
        y_ref[...] = _dot(hid.astype(MXU_DTYPE), w2b[...])

    @pl.when(nvalid_ref[i] == 0)
    def _():
        y_ref[...] = jnp.zeros_like(y_ref)


def _combine_kernel(dest_ref, y_hbm, x_ref, gate_ref, g2_ref, fg_ref, o_ref, ybuf, sem, *, tm, T, final):
    base = pl.program_id(0) * tm

    def issue(t, _):
        for k in range(2):
            d = dest_ref[k * T + base + t]
            pltpu.make_async_copy(y_hbm.at[pl.ds(d, 1)], ybuf.at[k, pl.ds(t, 1)], sem).start()
        return 0
    lax.fori_loop(0, tm, issue, 0, unroll=DMA_UNROLL)
    for k in range(2):
        pltpu.make_async_copy(y_hbm.at[pl.ds(0, tm)], ybuf.at[k], sem).wait()
    g = gate_ref[...]
    x = x_ref[...] + g2_ref[0] * (g[:, 0:1] * ybuf[0] + g[:, 1:2] * ybuf[1])
    if final:
        x = _rms(x) * fg_ref[...]
    o_ref[...] = x


def _moe_ffn(h2, slot, counts, layer, w1, w3, w2):
    T, D = h2.shape
    P = 2 * T + N_EXPERTS * MOE_BLOCK
    nblk = P // MOE_BLOCK
    tm = min(ROW_TILE, T)
    counts = counts[0, :N_EXPERTS].astype(I32)
    padded = (counts + MOE_BLOCK - 1) // MOE_BLOCK * MOE_BLOCK
    pad_end = jnp.cumsum(padded)
    pad_start = pad_end - padded
    experts = jnp.arange(N_EXPERTS, dtype=I32)
    e, rank = slot[0:2].astype(I32), slot[2:4].astype(I32)
    seg_start = jnp.sum(jnp.where(e[..., None] == experts, pad_start, 0), axis=-1)
    dest = (seg_start + rank).reshape(-1)
    blk_start = jnp.arange(nblk, dtype=I32) * MOE_BLOCK
    blk_e = jnp.minimum(jnp.sum((pad_end[None, :] <= blk_start[:, None]).astype(I32), axis=1), N_EXPERTS - 1)
    seg_end = jnp.sum(jnp.where(blk_e[:, None] == experts, pad_start + counts, 0), axis=-1)
    nvalid = jnp.clip(seg_end - blk_start, 0, MOE_BLOCK).astype(I32)

    xs = pl.pallas_call(
        functools.partial(_dispatch_kernel, tm=tm, T=T),
        grid_spec=pltpu.PrefetchScalarGridSpec(
            num_scalar_prefetch=1, grid=(T // tm,),
            in_specs=[pl.BlockSpec((tm, D), lambda i, d: (i, 0)), pl.BlockSpec(memory_space=pl.ANY)],
            out_specs=pl.BlockSpec(memory_space=pl.ANY),
            scratch_shapes=[pltpu.SemaphoreType.DMA]),
        out_shape=jax.ShapeDtypeStruct((P, D), F32),
        input_output_aliases={2: 0},
        compiler_params=_params(("arbitrary",)),
        name="moe_dispatch",
    )(dest, h2, jnp.zeros((P, D), F32))

    ys = pl.pallas_call(
        _ffn_kernel,
        grid_spec=pltpu.PrefetchScalarGridSpec(
            num_scalar_prefetch=2, grid=(nblk,),
            in_specs=[pl.BlockSpec((MOE_BLOCK, D), lambda i, be, nv: (i, 0)),
                      pl.BlockSpec((1, D, D_EXPERT), lambda i, be, nv: (be[i], 0, 0)),
                      pl.BlockSpec((1, D, D_EXPERT), lambda i, be, nv: (be[i], 0, 0)),
                      pl.BlockSpec((1, D_EXPERT, D), lambda i, be, nv: (be[i], 0, 0))],
            out_specs=pl.BlockSpec((MOE_BLOCK, D), lambda i, be, nv: (i, 0)),
            scratch_shapes=[pltpu.VMEM((D, D_EXPERT), MXU_DTYPE), pltpu.VMEM((D, D_EXPERT), MXU_DTYPE),
                            pltpu.VMEM((D_EXPERT, D), MXU_DTYPE)]),
        out_shape=jax.ShapeDtypeStruct((P, D), F32),
        compiler_params=_params(("arbitrary",)),
        name="moe_ffn",
    )(blk_e + layer * N_EXPERTS, nvalid, xs, *[w.reshape((-1,) + w.shape[2:]) for w in (w1, w3, w2)])
    return ys, dest


def _moe_combine(x1, ys, dest, gates, g2, final_g):
    B, S, D = x1.shape
    T = B * S
    tm = min(ROW_TILE, S)
    final = final_g is not None
    fg = (final_g if final else jnp.ones((D,), F32)).reshape(1, D)
    out = pl.pallas_call(
        functools.partial(_combine_kernel, tm=tm, T=T, final=final),
        grid_spec=pltpu.PrefetchScalarGridSpec(
            num_scalar_prefetch=1, grid=(T // tm,),
            in_specs=[pl.BlockSpec(memory_space=pl.ANY),
                      pl.BlockSpec((tm, D), lambda i, d: (i, 0)),
                      pl.BlockSpec((tm, 8), lambda i, d: (i, 0)),
                      pl.BlockSpec((1, 1, D), lambda i, d: (i // (S // tm), 0, 0)),
                      pl.BlockSpec((1, D), lambda i, d: (0, 0))],
            out_specs=pl.BlockSpec((tm, D), lambda i, d: (i, 0)),
            scratch_shapes=[pltpu.VMEM((2, tm, D), F32), pltpu.SemaphoreType.DMA]),
        out_shape=jax.ShapeDtypeStruct((T, D), F32),
        compiler_params=_params(("arbitrary",)),
        name="moe_combine",
    )(dest, ys, x1.reshape(T, D), gates.reshape(T, 8), g2, fg)
    return out.reshape(B, S, D)


def kernel(x, c, ada_w, ada_b, norm1_g, w_in, conv_w, conv_b, conv_ln_g, conv_ln_b, moba_norm_g, dsa_norm_g, w_out, norm2_g, router_group_w, router_group_b, router_expert_w, router_expert_b, expert_w1, expert_w3, expert_w2, final_g):
    B, S, D = x.shape
    L = ada_w.shape[0]
    mod = _modulation(c, ada_w, ada_b)
    for l in range(L):
        sh1, sc1, g1, sh2, sc2, g2 = [m[:, None, :] for m in jnp.split(mod[l], 6, axis=-1)]
        qmT, km, vmT, kmean, u, qdT, kd, vdT, qiT, ki, wT = _inproj(
            x, norm1_g[l], sc1, sh1, _split_w_in(w_in[l]))
        y_moba = _moba(qmT, km, vmT, kmean)
        y_dsa = _dsa(qdT, kd, vdT, qiT, ki, wT)
        p = dict(conv_w=conv_w[l], conv_b=conv_b[l], conv_ln_g=conv_ln_g[l], conv_ln_b=conv_ln_b[l],
                 moba_norm_g=moba_norm_g[l], dsa_norm_g=dsa_norm_g[l], w_out=w_out[l], norm2_g=norm2_g[l],
                 router_group_w=router_group_w[l], router_group_b=router_group_b[l],
                 router_expert_w=router_expert_w[l], router_expert_b=router_expert_b[l])
        x1, h2, slot, gates, counts = _mix(x, y_moba, y_dsa, u, p, g1, sc2, sh2)
        ys, dest = _moe_ffn(h2.reshape(B * S, D), slot, counts, l, expert_w1, expert_w3, expert_w2)
        x = _moe_combine(x1, ys, dest, gates, g2, final_g if l == L - 1 else None)
    return x
```
